```python
import jax, jax.numpy as jnp
from jax import lax
import numpy as np

D_MODEL = 2048
BATCH = 4
SEQ = 4096
DEPTH = 2

HEAD_DIM = 128
HA = 8
GA = 2
RA = HA // GA
HB = 8
WIN = 128
BLK = 128
GRID_W = 64
NA_KH = 8
NA_KW = 16
ROT_DIM = HEAD_DIM // 4
ROPE_THETA = 500000.0
D_FF = 5632
CONV_W = 3
EPS = 1e-6
NEG_INF = -1e30
QA_W = HA * HEAD_DIM
KA_W = GA * HEAD_DIM
QB_W = HB * HEAD_DIM
IN_SIZES = (QA_W, KA_W, KA_W, QB_W, QB_W, QB_W, D_MODEL, D_MODEL)
IN_COLS = QA_W + 2 * KA_W + 3 * QB_W + 2 * D_MODEL

kernel_name = "hybrid_window_gqa_neighbourhood_convffn_adaln"


def rmsnorm(x, g):
    xf = x.astype(jnp.float32)
    y = xf * lax.rsqrt(jnp.mean(xf * xf, axis=-1, keepdims=True) + EPS)
    return (y * g.astype(jnp.float32)).astype(x.dtype)


def split_cols(p):
    pts, acc = [], 0
    for s in IN_SIZES[:-1]:
        acc += s
        pts.append(acc)
    return jnp.split(p, pts, axis=-1)


def rotary_partial(x, pos):
    half = ROT_DIM // 2
    inv = jnp.float32(ROPE_THETA) ** (-jnp.arange(0, ROT_DIM, 2, dtype=jnp.float32) / ROT_DIM)
    ang = pos[:, None] * inv[None, :]
    cos = jnp.cos(ang)[None, :, None, :].astype(x.dtype)
    sin = jnp.sin(ang)[None, :, None, :].astype(x.dtype)
    x1, x2, xp = x[..., :half], x[..., half:ROT_DIM], x[..., ROT_DIM:]
    return jnp.concatenate([x1 * cos - x2 * sin, x2 * cos + x1 * sin, xp], axis=-1)


def window_attention(q, k, v, sink):
    B, S = q.shape[0], q.shape[1]
    nb = S // BLK
    qb = q.astype(jnp.float32).reshape(B, nb, BLK, GA, RA, HEAD_DIM)

    def band(t):
        tp = jnp.pad(t.astype(jnp.float32), ((0, 0), (BLK, BLK), (0, 0), (0, 0)))
        tp = tp.reshape(B, nb + 2, BLK, GA, HEAD_DIM)
        return jnp.concatenate([tp[:, :-2], tp[:, 1:-1], tp[:, 2:]], axis=2)

    kw, vw = band(k), band(v)
    qi = jnp.arange(BLK)
    ki = jnp.arange(3 * BLK) - BLK
    rel = ki[None, :] - qi[:, None]
    kpos = jnp.arange(nb)[:, None] * BLK + ki[None, :]
    mask = (jnp.abs(rel) <= WIN)[None] & ((kpos >= 0) & (kpos < S))[:, None, :]
    s = jnp.einsum('bnqgrd,bnkgd->bgrnqk', qb, kw) * (HEAD_DIM ** -0.5)
    s = jnp.where(mask[None, None, None], s, NEG_INF)
    snk = sink.astype(jnp.float32).reshape(GA, RA)[None, :, :, None, None, None]
    m = jnp.maximum(jnp.max(s, axis=-1, keepdims=True), snk)
    p = jnp.exp(s - m)
    p = p / (jnp.sum(p, axis=-1, keepdims=True) + jnp.exp(snk - m))
    o = jnp.einsum('bgrnqk,bnkgd->bnqgrd', p, vw)
    return o.reshape(B, S, HA * HEAD_DIM).astype(q.dtype)


def neighbourhood_attention(q, k, v, bias_tab):
    B, S = q.shape[0], q.shape[1]
    rows = S // GRID_W
    kh = min(NA_KH, rows)
    r = jnp.arange(rows)
    row_start = jnp.clip(r - kh // 2, 0, rows - kh)
    key_rows = row_start[:, None] + jnp.arange(kh)[None, :]
    qg = q.astype(jnp.float32).reshape(B, rows, GRID_W, HB, HEAD_DIM)
    kg = k.astype(jnp.float32).reshape(B, rows, GRID_W, HB, HEAD_DIM)[:, key_rows]
    vg = v.astype(jnp.float32).reshape(B, rows, GRID_W, HB, HEAD_DIM)[:, key_rows]
    s = jnp.einsum('brqhd,brikhd->bhrqik', qg, kg) * (HEAD_DIM ** -0.5)
    col = jnp.arange(GRID_W)
    col_start = jnp.clip(col - NA_KW // 2, 0, GRID_W - NA_KW)
    col_mask = (col[None, :] >= col_start[:, None]) & (col[None, :] < col_start[:, None] + NA_KW)
    dr_idx = key_rows - r[:, None] + (NA_KH - 1)
    dc_idx = jnp.clip(col[None, :] - col[:, None] + (NA_KW - 1), 0, 2 * NA_KW - 2)
    bias = bias_tab.astype(jnp.float32)[:, dr_idx[:, None, :, None], dc_idx[None, :, None, :]]
    s = jnp.where(col_mask[None, None, None, :, None, :], s + bias[None], NEG_INF)
    p = jax.nn.softmax(s.reshape(B, HB, rows, GRID_W, kh * GRID_W), axis=-1)
    p = p.reshape(B, HB, rows, GRID_W, kh, GRID_W)
    o = jnp.einsum('bhrqik,brikhd->brqhd', p, vg)
    return o.reshape(B, S, HB * HEAD_DIM).astype(q.dtype)


def depthwise_conv(u, w, b):
    up = jnp.pad(u, ((0, 0), (1, 1), (0, 0)))
    return up[:, :-2] * w[0] + up[:, 1:-1] * w[1] + up[:, 2:] * w[2] + b


def setup_inputs(seed: int = 0) -> dict:
    key = jax.random.key(seed)
    ks = jax.random.split(key, 22)

    def nrm(k, shape, scale):
        return jax.random.normal(k, shape, jnp.float32) * scale

    L, D = DEPTH, D_MODEL
    return {
        "x": nrm(ks[0], (BATCH, SEQ, D), 1.0),
        "c": nrm(ks[1], (BATCH, D), 1.0),
        "ada_w": nrm(ks[2], (L, D, 6 * D), 0.5 * D ** -0.5),
        "ada_b": nrm(ks[3], (L, 6 * D), 0.02),
        "norm_mix": 1.0 + nrm(ks[4], (L, D), 0.05),
        "norm_ffn": 1.0 + nrm(ks[5], (L, D), 0.05),
        "w_in": nrm(ks[6], (L, D, IN_COLS), D ** -0.5),
        "qn_a": 1.0 + nrm(ks[7], (L, HEAD_DIM), 0.05),
        "kn_a": 1.0 + nrm(ks[8], (L, HEAD_DIM), 0.05),
        "qn_b": 1.0 + nrm(ks[9], (L, HEAD_DIM), 0.05),
        "kn_b": 1.0 + nrm(ks[10], (L, HEAD_DIM), 0.05),
        "sink_a": nrm(ks[11], (L, HA), 1.0),
        "rel_bias_b": nrm(ks[12], (L, HB, 2 * NA_KH - 1, 2 * NA_KW - 1), 0.5),
        "w_proj_a": nrm(ks[13], (L, QA_W, D), QA_W ** -0.5),
        "w_proj_b": nrm(ks[14], (L, QB_W, D), QB_W ** -0.5),
        "w_out": nrm(ks[15], (L, D, D), D ** -0.5),
        "w_up": nrm(ks[16], (L, D, 2 * D_FF), D ** -0.5),
        "conv_w": nrm(ks[17], (L, CONV_W, 2 * D_FF), CONV_W ** -0.5),
        "conv_b": nrm(ks[18], (L, 2 * D_FF), 0.02),
        "w_down": nrm(ks[19], (L, D_FF, D), D_FF ** -0.5),
    }


def reference(x, c, ada_w, ada_b, norm_mix, norm_ffn, w_in, qn_a, kn_a, qn_b, kn_b,
              sink_a, rel_bias_b, w_proj_a, w_proj_b, w_out, w_up, conv_w, conv_b, w_down):
    B, S, _ = x.shape
    pos = jnp.arange(S, dtype=jnp.float32)
    c_act = jax.nn.silu(c)
    for l in range(DEPTH):
        mod = c_act @ ada_w[l] + ada_b[l]
        sh_a, sc_a, gt_a, sh_m, sc_m, gt_m = [t[:, None, :] for t in jnp.split(mod, 6, axis=-1)]

        h = rmsnorm(x, norm_mix[l]) * (1.0 + sc_a) + sh_a
        qa, ka, va, qb, kb, vb, ga, gb = split_cols(h @ w_in[l])
        qa = rotary_partial(rmsnorm(qa.reshape(B, S, HA, HEAD_DIM), qn_a[l]), pos)
        ka = rotary_partial(rmsnorm(ka.reshape(B, S, GA, HEAD_DIM), kn_a[l]), pos)
        va = va.reshape(B, S, GA, HEAD_DIM)
        ya = window_attention(qa, ka, va, sink_a[l]) @ w_proj_a[l]
        qb = rmsnorm(qb.reshape(B, S, HB, HEAD_DIM), qn_b[l])
        kb = rmsnorm(kb.reshape(B, S, HB, HEAD_DIM), kn_b[l])
        vb = vb.reshape(B, S, HB, HEAD_DIM)
        yb = neighbourhood_attention(qb, kb, vb, rel_bias_b[l]) @ w_proj_b[l]
        merged = jax.nn.sigmoid(ga) * ya + jax.nn.sigmoid(gb) * yb
        x = x + gt_a * (merged @ w_out[l])

        h = rmsnorm(x, norm_ffn[l]) * (1.0 + sc_m) + sh_m
        u = depthwise_conv(h @ w_up[l], conv_w[l], conv_b[l])
        g, v = jnp.split(u, 2, axis=-1)
        x = x + gt_m * ((jax.nn.silu(g) * v) @ w_down[l])
    return x
```

```python
import functools

import jax
import jax.numpy as jnp
from jax import lax
from jax.experimental import pallas as pl
from jax.experimental.pallas import tpu as pltpu

F32 = jnp.float32
BF16 = jnp.bfloat16

D_MODEL = 2048
HEAD_DIM = 128
HA = 8
GA = 2
RA = HA // GA
HB = 8
WIN = 128
GRID_W = 64
NA_KH = 8
NA_KW = 16
ROT_DIM = HEAD_DIM // 4
ROPE_THETA = 500000.0
D_FF = 5632
EPS = 1e-6
NEG_INF = -1e30
QK_SCALE = HEAD_DIM ** -0.5
IN_COLS = HA * HEAD_DIM + 2 * GA * HEAD_DIM + 3 * HB * HEAD_DIM + 2 * D_MODEL

COL_QA = 0
COL_KA = COL_QA + HA
COL_VA = COL_KA + GA
COL_QB = COL_VA + GA
COL_KB = COL_QB + HB
COL_VB = COL_KB + HB
COL_GA = COL_VB + HB
COL_GB = COL_GA + D_MODEL // HEAD_DIM

VMEM_LIMIT = 56 * 1024 * 1024
ADA_TN = 1024
INPROJ_TM = 1024
INPROJ_TN = 512
ATT_TQ = 512
OUT_TM = 512
OUT_TC = 512
FFN_TM = 512
FFN_TF = 512
HALO = 16
N_DR = 2 * NA_KH - 1
N_DC = 2 * NA_KW - 1


def _cparams(n_axes):
    return pltpu.CompilerParams(dimension_semantics=("arbitrary",) * n_axes,
                                vmem_limit_bytes=VMEM_LIMIT)


def _sigmoid(x):
    return 1.0 / (1.0 + jnp.exp(-x))


def _rms(x):
    return x * lax.rsqrt(jnp.mean(x * x, axis=-1, keepdims=True) + EPS)


def _ada_kernel(c_ref, w_ref, b_ref, o_ref):
    c = c_ref[...]
    c_act = (c * _sigmoid(c)).astype(BF16)
    o_ref[...] = jnp.dot(c_act, w_ref[...].astype(BF16), preferred_element_type=F32) + b_ref[...]


def _ada_call(c_pad, ada_w, ada_b):
    n_layers, d, six_d = ada_w.shape
    rows = c_pad.shape[0]
    return pl.pallas_call(
        _ada_kernel,
        grid=(n_layers, six_d // ADA_TN),
        in_specs=[
            pl.BlockSpec((rows, d), lambda l, n: (0, 0)),
            pl.BlockSpec((None, d, ADA_TN), lambda l, n: (l, 0, n)),
            pl.BlockSpec((None, 1, ADA_TN), lambda l, n: (l, 0, n)),
        ],
        out_specs=pl.BlockSpec((None, rows, ADA_TN), lambda l, n: (l, 0, n)),
        out_shape=jax.ShapeDtypeStruct((n_layers, rows, six_d), F32),
        compiler_params=_cparams(2),
        name="ada_mod",
    )(c_pad, ada_w, ada_b.reshape(n_layers, 1, six_d))


def _rope(a, cos, sin_up, sin_dn):
    half = ROT_DIM // 2
    return (a * cos + pltpu.roll(a, half, 1) * sin_up
            + pltpu.roll(a, HEAD_DIM - half, 1) * sin_dn)


def _inproj_kernel(x_ref, g_ref, sc_ref, sh_ref, w_ref, cos_ref, sup_ref, sdn_ref,
                   qna_ref, kna_ref, qnb_ref, knb_ref, o_ref, h_ref, acc_ref):
    n = pl.program_id(1)
    heads_per_tile = INPROJ_TN // HEAD_DIM

    @pl.when(n == 0)
    def _():
        y = _rms(x_ref[...]) * g_ref[...]
        h_ref[...] = (y * (1.0 + sc_ref[...]) + sh_ref[...]).astype(BF16)

    acc_ref[...] = jnp.dot(h_ref[...], w_ref[...], preferred_element_type=F32)

    def head(k):
        return acc_ref[:, k * HEAD_DIM:(k + 1) * HEAD_DIM]

    def put(k, val):
        o_ref[:, k * HEAD_DIM:(k + 1) * HEAD_DIM] = val.astype(BF16)

    def rope(a):
        return _rope(a, cos_ref[...], sup_ref[...], sdn_ref[...])

    n_qa = HA // heads_per_tile
    t_kva = n_qa
    t_qb = COL_QB // heads_per_tile
    t_kb = COL_KB // heads_per_tile
    t_vb = COL_VB // heads_per_tile
    t_gate = COL_GA // heads_per_tile

    @pl.when(n < n_qa)
    def _():
        for k in range(heads_per_tile):
            put(k, rope(_rms(head(k)) * qna_ref[...]) * QK_SCALE)

    @pl.when(n == t_kva)
    def _():
        for k in range(GA):
            put(k, rope(_rms(head(k)) * kna_ref[...]))
        for k in range(GA, heads_per_tile):
            put(k, head(k))

    @pl.when((n >= t_qb) & (n < t_kb))
    def _():
        for k in range(heads_per_tile):
            put(k, _rms(head(k)) * qnb_ref[...] * QK_SCALE)

    @pl.when((n >= t_kb) & (n < t_vb))
    def _():
        for k in range(heads_per_tile):
            put(k, _rms(head(k)) * knb_ref[...])

    @pl.when((n >= t_vb) & (n < t_gate))
    def _():
        o_ref[...] = acc_ref[...].astype(BF16)

    @pl.when(n >= t_gate)
    def _():
        o_ref[...] = _sigmoid(acc_ref[...]).astype(BF16)


def _inproj_call(x2, g, sc, sh, w, cos_t, sup_t, sdn_t, qna, kna, qnb, knb, seq):
    t, d = x2.shape
    tm, tn = INPROJ_TM, INPROJ_TN
    assert 2 * GA * HEAD_DIM == tn and COL_QB * HEAD_DIM % tn == 0
    tiles_per_seq = seq // tm
    vec = lambda: pl.BlockSpec((1, HEAD_DIM), lambda i, n: (0, 0))
    tab = lambda: pl.BlockSpec((tm, HEAD_DIM), lambda i, n: (i % tiles_per_seq, 0))
    mod = lambda: pl.BlockSpec((None, 1, d), lambda i, n: (i // tiles_per_seq, 0, 0))
    return pl.pallas_call(
        _inproj_kernel,
        grid=(t // tm, IN_COLS // tn),
        in_specs=[
            pl.BlockSpec((tm, d), lambda i, n: (i, 0)),
            pl.BlockSpec((1, d), lambda i, n: (0, 0)),
            mod(), mod(),
            pl.BlockSpec((d, tn), lambda i, n: (0, n)),
            tab(), tab(), tab(),
            vec(), vec(), vec(), vec(),
        ],
        out_specs=pl.BlockSpec((tm, tn), lambda i, n: (i, n)),
        out_shape=jax.ShapeDtypeStruct((t, IN_COLS), BF16),
        scratch_shapes=[pltpu.VMEM((tm, d), BF16), pltpu.VMEM((tm, tn), F32)],
        compiler_params=_cparams(2),
        name="in_proj",
    )(x2, g, sc, sh, w, cos_t, sup_t, sdn_t, qna, kna, qnb, knb)


def _win_kernel(sink_ref, q_ref, k_ref, v_ref, o_ref, *, seq):
    g = pl.program_id(1)
    i = pl.program_id(2)
    band = 3 * WIN
    rows = RA * WIN
    row = lax.broadcasted_iota(jnp.int32, (rows, 1), 0)
    snk = jnp.full((rows, 1), sink_ref[g * RA + RA - 1], F32)
    for r in range(RA - 2, -1, -1):
        snk = jnp.where(row < (r + 1) * WIN, sink_ref[g * RA + r], snk)
    q_in_blk = row & (WIN - 1)
    col = lax.broadcasted_iota(jnp.int32, (1, band), 1)
    for j in range(ATT_TQ // WIN):
        blk = i * (ATT_TQ // WIN) + j
        start = pl.multiple_of(jnp.clip((blk - 1) * WIN, 0, seq - band), WIN)
        kw = k_ref[pl.ds(start, band), :]
        vw = v_ref[pl.ds(start, band), :]
        qs = jnp.concatenate(
            [q_ref[j * WIN:(j + 1) * WIN, r * HEAD_DIM:(r + 1) * HEAD_DIM] for r in range(RA)],
            axis=0)
        s = lax.dot_general(qs, kw, (((1,), (1,)), ((), ())), preferred_element_type=F32)
        rel = (start + col) - (blk * WIN + q_in_blk)
        s = jnp.where(jnp.abs(rel) <= WIN, s, NEG_INF)
        m = jnp.maximum(jnp.max(s, axis=-1, keepdims=True), snk)
        p = jnp.exp(s - m)
        denom = jnp.sum(p, axis=-1, keepdims=True) + jnp.exp(snk - m)
        o = jnp.dot(p.astype(BF16), vw, preferred_element_type=F32) / denom
        for r in range(RA):
            o_ref[j * WIN:(j + 1) * WIN, r * HEAD_DIM:(r + 1) * HEAD_DIM] = (
                o[r * WIN:(r + 1) * WIN].astype(BF16))


def _win_call(qkvg, sink, batch, seq):
    t = qkvg.shape[0]
    tq = ATT_TQ
    per_seq = seq // tq
    gw = RA * HEAD_DIM
    return pl.pallas_call(
        functools.partial(_win_kernel, seq=seq),
        grid=(batch, GA, per_seq),
        in_specs=[
            pl.BlockSpec(memory_space=pltpu.SMEM),
            pl.BlockSpec((tq, gw), lambda b, g, i: (b * per_seq + i, g)),
            pl.BlockSpec((seq, HEAD_DIM), lambda b, g, i: (b, COL_KA + g)),
            pl.BlockSpec((seq, HEAD_DIM), lambda b, g, i: (b, COL_VA + g)),
        ],
        out_specs=pl.BlockSpec((tq, gw), lambda b, g, i: (b * per_seq + i, g)),
        out_shape=jax.ShapeDtypeStruct((t, HA * HEAD_DIM), BF16),
        compiler_params=_cparams(3),
        name="win_attn",
    )(sink, qkvg, qkvg, qkvg)


def _nbr_kernel(tab_ref, q_ref, k_ref, v_ref, o_ref, bias_ref, tz_ref, *, grid_rows):
    h = pl.program_id(0)
    b = pl.program_id(1)
    i = pl.program_id(2)
    w = GRID_W
    keys = NA_KH * w

    @pl.when((b == 0) & (i == 0))
    def _():
        qi = lax.broadcasted_iota(jnp.int32, (w, 2 * w), 0)
        lane = lax.broadcasted_iota(jnp.int32, (w, 2 * w), 1)
        kc = lane & (w - 1)
        diff = kc - qi
        c0 = jnp.clip(qi - NA_KW // 2, 0, w - NA_KW)
        valid = (kc >= c0) & (kc < c0 + NA_KW)
        base = h * (N_DR * N_DC)
        for dr in range(N_DR):
            tz = jnp.zeros((w, 2 * w), F32)
            for dc in range(N_DC):
                tz = jnp.where(diff == dc - (NA_KW - 1), tab_ref[base + dr * N_DC + dc], tz)
            tz_ref[dr] = jnp.where(valid, tz, NEG_INF)
        low = lane < w
        for delta in range(NA_KH):
            for c in range(NA_KH // 2):
                bias_ref[delta, :, c * 2 * w:(c + 1) * 2 * w] = jnp.where(
                    low, tz_ref[2 * c - delta + NA_KH - 1], tz_ref[2 * c + 1 - delta + NA_KH - 1])

    for rr in range(ATT_TQ // w):
        r = i * (ATT_TQ // w) + rr
        r0 = jnp.clip(r - NA_KH // 2, 0, grid_rows - NA_KH)
        delta = r - r0
        off = pl.multiple_of(r0 * w, w)
        kw = k_ref[pl.ds(off, keys), :]
        vw = v_ref[pl.ds(off, keys), :]
        qr = q_ref[rr * w:(rr + 1) * w, :]
        s = lax.dot_general(qr, kw, (((1,), (1,)), ((), ())), preferred_element_type=F32)
        s = s + bias_ref[delta]
        m = jnp.max(s, axis=-1, keepdims=True)
        p = jnp.exp(s - m)
        denom = jnp.sum(p, axis=-1, keepdims=True)
        o = jnp.dot(p.astype(BF16), vw, preferred_element_type=F32) / denom
        o_ref[rr * w:(rr + 1) * w, :] = o.astype(BF16)


def _nbr_call(qkvg, tab_flat, batch, seq):
    t = qkvg.shape[0]
    tq = ATT_TQ
    per_seq = seq // tq
    grid_rows = seq // GRID_W
    assert grid_rows >= NA_KH
    return pl.pallas_call(
        functools.partial(_nbr_kernel, grid_rows=grid_rows),
        grid=(HB, batch, per_seq),
        in_specs=[
            pl.BlockSpec(memory_space=pltpu.SMEM),
            pl.BlockSpec((tq, HEAD_DIM), lambda h, b, i: (b * per_seq + i, COL_QB + h)),
            pl.BlockSpec((seq, HEAD_DIM), lambda h, b, i: (b, COL_KB + h)),
            pl.BlockSpec((seq, HEAD_DIM), lambda h, b, i: (b, COL_VB + h)),
        ],
        out_specs=pl.BlockSpec((tq, HEAD_DIM), lambda h, b, i: (b * per_seq + i, h)),
        out_shape=jax.ShapeDtypeStruct((t, HB * HEAD_DIM), BF16),
        scratch_shapes=[pltpu.VMEM((NA_KH, GRID_W, NA_KH * GRID_W), F32),
                        pltpu.VMEM((N_DR, GRID_W, 2 * GRID_W), F32)],
        compiler_params=_cparams(3),
        name="nbr_attn",
    )(tab_flat, qkvg, qkvg, qkvg)


def _outproj_kernel(oa_ref, ob_ref, ga_ref, gb_ref, wpa_ref, wpb_ref, wo_ref, x_ref, gt_ref,
                    o_ref, acc_ref):
    c = pl.program_id(1)
    ya = jnp.dot(oa_ref[...], wpa_ref[...], preferred_element_type=F32)
    yb = jnp.dot(ob_ref[...], wpb_ref[...], preferred_element_type=F32)
    merged = (ga_ref[...].astype(F32) * ya + gb_ref[...].astype(F32) * yb).astype(BF16)
    part = jnp.dot(merged, wo_ref[...], preferred_element_type=F32)

    @pl.when(c == 0)
    def _():
        acc_ref[...] = part

    @pl.when(c > 0)
    def _():
        acc_ref[...] += part

    @pl.when(c == pl.num_programs(1) - 1)
    def _():
        o_ref[...] = x_ref[...] + gt_ref[...] * acc_ref[...]


def _outproj_call(oa, ob, qkvg, wpa, wpb, wo, x2, gt, seq):
    t, d = x2.shape
    tm, tc = OUT_TM, OUT_TC
    ga0 = COL_GA * HEAD_DIM // tc
    gb0 = COL_GB * HEAD_DIM // tc
    assert ga0 * tc == COL_GA * HEAD_DIM and gb0 * tc == COL_GB * HEAD_DIM
    tiles_per_seq = seq // tm
    return pl.pallas_call(
        _outproj_kernel,
        grid=(t // tm, d // tc),
        in_specs=[
            pl.BlockSpec((tm, oa.shape[1]), lambda i, c: (i, 0)),
            pl.BlockSpec((tm, ob.shape[1]), lambda i, c: (i, 0)),
            pl.BlockSpec((tm, tc), lambda i, c: (i, ga0 + c)),
            pl.BlockSpec((tm, tc), lambda i, c: (i, gb0 + c)),
            pl.BlockSpec((wpa.shape[0], tc), lambda i, c: (0, c)),
            pl.BlockSpec((wpb.shape[0], tc), lambda i, c: (0, c)),
            pl.BlockSpec((tc, d), lambda i, c: (c, 0)),
            pl.BlockSpec((tm, d), lambda i, c: (i, 0)),
            pl.BlockSpec((None, 1, d), lambda i, c: (i // tiles_per_seq, 0, 0)),
        ],
        out_specs=pl.BlockSpec((tm, d), lambda i, c: (i, 0)),
        out_shape=jax.ShapeDtypeStruct((t, d), F32),
        scratch_shapes=[pltpu.VMEM((tm, d), F32)],
        compiler_params=_cparams(2),
        name="out_proj",
    )(oa, ob, qkvg, qkvg, wpa, wpb, wo, x2, gt)


def _ffn_kernel(x_ref, xp_ref, xn_ref, g_ref, sc_ref, sh_ref, gt_ref, wg_ref, wv_ref,
                cwg_ref, cwv_ref, cbg_ref, cbv_ref, wd_ref, o_ref, h_ref, acc_ref, *, tiles_per_seq):
    i = pl.program_id(0)
    j = pl.program_id(1)
    tm = FFN_TM
    ext = tm + 2 * HALO

    @pl.when(j == 0)
    def _():
        def modulated(x):
            return _rms(x) * g_ref[...] * (1.0 + sc_ref[...]) + sh_ref[...]

        pos = i % tiles_per_seq
        keep_prev = jnp.where(pos == 0, 0.0, 1.0)
        keep_next = jnp.where(pos == tiles_per_seq - 1, 0.0, 1.0)
        h_ref[0:HALO, :] = (modulated(xp_ref[...]) * keep_prev).astype(BF16)
        h_ref[HALO:HALO + tm, :] = modulated(x_ref[...]).astype(BF16)
        h_ref[HALO + tm:ext, :] = (modulated(xn_ref[...]) * keep_next).astype(BF16)

    h = h_ref[...]

    def conv(u, cw_ref, cb_ref):
        prev = pltpu.roll(u, 1, 0)[HALO:HALO + tm]
        nxt = pltpu.roll(u, ext - 1, 0)[HALO:HALO + tm]
        return (prev * cw_ref[0:1, :] + u[HALO:HALO + tm] * cw_ref[1:2, :]
                + nxt * cw_ref[2:3, :] + cb_ref[...])

    gate = conv(jnp.dot(h, wg_ref[...], preferred_element_type=F32), cwg_ref, cbg_ref)
    val = conv(jnp.dot(h, wv_ref[...], preferred_element_type=F32), cwv_ref, cbv_ref)
    act = (gate * _sigmoid(gate) * val).astype(BF16)
    part = jnp.dot(act, wd_ref[...], preferred_element_type=F32)

    @pl.when(j == 0)
    def _():
        acc_ref[...] = part

    @pl.when(j > 0)
    def _():
        acc_ref[...] += part

    @pl.when(j == pl.num_programs(1) - 1)
    def _():
        o_ref[...] = x_ref[...] + gt_ref[...] * acc_ref[...]


def _ffn_call(x2, g, sc, sh, gt, w_up, conv_w, conv_b, w_down, seq):
    t, d = x2.shape
    tm, tf = FFN_TM, FFN_TF
    d_ff = w_down.shape[0]
    nf = d_ff // tf
    tiles_per_seq = seq // tm
    halo_per_tile = tm // HALO
    n_halo = t // HALO
    mod = lambda: pl.BlockSpec((None, 1, d), lambda i, j: (i // tiles_per_seq, 0, 0))
    return pl.pallas_call(
        functools.partial(_ffn_kernel, tiles_per_seq=tiles_per_seq),
        grid=(t // tm, nf),
        in_specs=[
            pl.BlockSpec((tm, d), lambda i, j: (i, 0)),
            pl.BlockSpec((HALO, d), lambda i, j: (jnp.maximum(i * halo_per_tile - 1, 0), 0)),
            pl.BlockSpec((HALO, d),
                         lambda i, j: (jnp.minimum((i + 1) * halo_per_tile, n_halo - 1), 0)),
            pl.BlockSpec((1, d), lambda i, j: (0, 0)),
            mod(), mod(), mod(),
            pl.BlockSpec((d, tf), lambda i, j: (0, j)),
            pl.BlockSpec((d, tf), lambda i, j: (0, j + nf)),
            pl.BlockSpec((conv_w.shape[0], tf), lambda i, j: (0, j)),
            pl.BlockSpec((conv_w.shape[0], tf), lambda i, j: (0, j + nf)),
            pl.BlockSpec((1, tf), lambda i, j: (0, j)),
            pl.BlockSpec((1, tf), lambda i, j: (0, j + nf)),
            pl.BlockSpec((tf, d), lambda i, j: (j, 0)),
        ],
        out_specs=pl.BlockSpec((tm, d), lambda i, j: (i, 0)),
        out_shape=jax.ShapeDtypeStruct((t, d), F32),
        scratch_shapes=[pltpu.VMEM((tm + 2 * HALO, d), BF16), pltpu.VMEM((tm, d), F32)],
        compiler_params=_cparams(2),
        name="conv_ffn",
    )(x2, x2, x2, g, sc, sh, gt, w_up, w_up, conv_w, conv_w, conv_b, conv_b, w_down)


def _rope_tables(seq):
    half = ROT_DIM // 2
    pos = jnp.arange(seq, dtype=F32)
    inv = jnp.float32(ROPE_THETA) ** (-jnp.arange(0, ROT_DIM, 2, dtype=F32) / ROT_DIM)
    ang = pos[:, None] * inv[None, :]
    cos, sin = jnp.cos(ang), jnp.sin(ang)
    rest = HEAD_DIM - ROT_DIM
    cos_t = jnp.concatenate([cos, cos, jnp.ones((seq, rest), F32)], axis=1)
    sin_up = jnp.concatenate([jnp.zeros((seq, half), F32), sin, jnp.zeros((seq, rest), F32)], axis=1)
    sin_dn = jnp.concatenate([-sin, jnp.zeros((seq, HEAD_DIM - half), F32)], axis=1)
    return cos_t, sin_up, sin_dn


def kernel(x, c, ada_w, ada_b, norm_mix, norm_ffn, w_in, qn_a, kn_a, qn_b, kn_b, sink_a,
           rel_bias_b, w_proj_a, w_proj_b, w_out, w_up, conv_w, conv_b, w_down):
    batch, seq, d = x.shape
    n_layers = ada_w.shape[0]
    t = batch * seq
    x2 = x.reshape(t, d)

    c_pad = jnp.pad(c, ((0, 8 - batch % 8 if batch % 8 else 0), (0, 0)))
    mod = _ada_call(c_pad, ada_w, ada_b)[:, :batch]
    mod = mod.reshape(n_layers, batch, 6, 1, d)
    cos_t, sin_up, sin_dn = _rope_tables(seq)

    for l in range(n_layers):
        sh_a, sc_a, gt_a, sh_m, sc_m, gt_m = [mod[l, :, k] for k in range(6)]
        qkvg = _inproj_call(
            x2, norm_mix[l][None], sc_a, sh_a, w_in[l].astype(BF16), cos_t, sin_up, sin_dn,
            qn_a[l][None], kn_a[l][None], qn_b[l][None], kn_b[l][None], seq)
        oa = _win_call(qkvg, sink_a[l], batch, seq)
        ob = _nbr_call(qkvg, rel_bias_b[l].reshape(-1), batch, seq)
        x2 = _outproj_call(oa, ob, qkvg, w_proj_a[l].astype(BF16), w_proj_b[l].astype(BF16),
                           w_out[l].astype(BF16), x2, gt_a, seq)
        x2 = _ffn_call(x2, norm_ffn[l][None], sc_m, sh_m, gt_m, w_up[l].astype(BF16),
                       conv_w[l], conv_b[l][None], w_down[l].astype(BF16), seq)
    return x2.reshape(batch, seq, d)
```

```python
import functools

import jax
import jax.numpy as jnp
from jax import lax
from jax.experimental import pallas as pl
from jax.experimental.pallas import tpu as pltpu

F32 = jnp.float32
BF16 = jnp.bfloat16

D_MODEL = 2048
HEAD_DIM = 128
HA = 8
GA = 2
RA = HA // GA
HB = 8
WIN = 128
GRID_W = 64
NA_KH = 8
NA_KW = 16
ROT_DIM = HEAD_DIM // 4
ROPE_THETA = 500000.0
D_FF = 5632
EPS = 1e-6
NEG_INF = -1e30
QK_SCALE = HEAD_DIM ** -0.5
IN_COLS = HA * HEAD_DIM + 2 * GA * HEAD_DIM + 3 * HB * HEAD_DIM + 2 * D_MODEL

COL_QA = 0
COL_KA = COL_QA + HA
COL_VA = COL_KA + GA
COL_QB = COL_VA + GA
COL_KB = COL_QB + HB
COL_VB = COL_KB + HB
COL_GA = COL_VB + HB
COL_GB = COL_GA + D_MODEL // HEAD_DIM

VMEM_LIMIT = 56 * 1024 * 1024
ADA_TN = 1024
INPROJ_TM = 1024
INPROJ_TN = 512
ATT_TQ = 512
OUT_TM = 512
OUT_TC = 512
FFN_TM = 512
FFN_TF = 512
FFN_SUB = 256
HALO = 16
N_DR = 2 * NA_KH - 1
N_DC = 2 * NA_KW - 1


def _cparams(n_axes):
    return pltpu.CompilerParams(dimension_semantics=("arbitrary",) * n_axes,
                                vmem_limit_bytes=VMEM_LIMIT)


def _sigmoid(x):
    return 1.0 / (1.0 + jnp.exp(-x))


def _rms(x):
    return x * lax.rsqrt(jnp.mean(x * x, axis=-1, keepdims=True) + EPS)


def _ada_kernel(c_ref, w_ref, b_ref, o_ref):
    c = c_ref[...]
    c_act = (c * _sigmoid(c)).astype(BF16)
    o_ref[...] = jnp.dot(c_act, w_ref[...].astype(BF16), preferred_element_type=F32) + b_ref[...]


def _ada_call(c_pad, ada_w, ada_b):
    n_layers, d, six_d = ada_w.shape
    rows = c_pad.shape[0]
    return pl.pallas_call(
        _ada_kernel,
        grid=(n_layers, six_d // ADA_TN),
        in_specs=[
            pl.BlockSpec((rows, d), lambda l, n: (0, 0)),
            pl.BlockSpec((None, d, ADA_TN), lambda l, n: (l, 0, n)),
            pl.BlockSpec((None, 1, ADA_TN), lambda l, n: (l, 0, n)),
        ],
        out_specs=pl.BlockSpec((None, rows, ADA_TN), lambda l, n: (l, 0, n)),
        out_shape=jax.ShapeDtypeStruct((n_layers, rows, six_d), F32),
        compiler_params=_cparams(2),
        name="ada_mod",
    )(c_pad, ada_w, ada_b.reshape(n_layers, 1, six_d))


def _rope(a, cos, sin_up, sin_dn):
    half = ROT_DIM // 2
    return (a * cos + pltpu.roll(a, half, 1) * sin_up
            + pltpu.roll(a, HEAD_DIM - half, 1) * sin_dn)


def _inproj_kernel(x_ref, g_ref, sc_ref, sh_ref, w_ref, cos_ref, sup_ref, sdn_ref,
                   qna_ref, kna_ref, qnb_ref, knb_ref, o_ref, h_ref, acc_ref):
    n = pl.program_id(1)
    heads_per_tile = INPROJ_TN // HEAD_DIM

    @pl.when(n == 0)
    def _():
        y = _rms(x_ref[...]) * g_ref[...]
        h_ref[...] = (y * (1.0 + sc_ref[...]) + sh_ref[...]).astype(BF16)

    acc_ref[...] = jnp.dot(h_ref[...], w_ref[...], preferred_element_type=F32)

    def head(k):
        return acc_ref[:, k * HEAD_DIM:(k + 1) * HEAD_DIM]

    def put(k, val):
        o_ref[:, k * HEAD_DIM:(k + 1) * HEAD_DIM] = val.astype(BF16)

    def rope(a):
        return _rope(a, cos_ref[...], sup_ref[...], sdn_ref[...])

    n_qa = HA // heads_per_tile
    t_kva = n_qa
    t_qb = COL_QB // heads_per_tile
    t_kb = COL_KB // heads_per_tile
    t_vb = COL_VB // heads_per_tile
    t_gate = COL_GA // heads_per_tile

    @pl.when(n < n_qa)
    def _():
        for k in range(heads_per_tile):
            put(k, rope(_rms(head(k)) * qna_ref[...]) * QK_SCALE)

    @pl.when(n == t_kva)
    def _():
        for k in range(GA):
            put(k, rope(_rms(head(k)) * kna_ref[...]))
        for k in range(GA, heads_per_tile):
            put(k, head(k))

    @pl.when((n >= t_qb) & (n < t_kb))
    def _():
        for k in range(heads_per_tile):
            put(k, _rms(head(k)) * qnb_ref[...] * QK_SCALE)

    @pl.when((n >= t_kb) & (n < t_vb))
    def _():
        for k in range(heads_per_tile):
            put(k, _rms(head(k)) * knb_ref[...])

    @pl.when((n >= t_vb) & (n < t_gate))
    def _():
        o_ref[...] = acc_ref[...].astype(BF16)

    @pl.when(n >= t_gate)
    def _():
        o_ref[...] = _sigmoid(acc_ref[...]).astype(BF16)


def _inproj_call(x2, g, sc, sh, w, cos_t, sup_t, sdn_t, qna, kna, qnb, knb, seq):
    t, d = x2.shape
    tm, tn = INPROJ_TM, INPROJ_TN
    assert 2 * GA * HEAD_DIM == tn and COL_QB * HEAD_DIM % tn == 0
    tiles_per_seq = seq // tm
    vec = lambda: pl.BlockSpec((1, HEAD_DIM), lambda i, n: (0, 0))
    tab = lambda: pl.BlockSpec((tm, HEAD_DIM), lambda i, n: (i % tiles_per_seq, 0))
    mod = lambda: pl.BlockSpec((None, 1, d), lambda i, n: (i // tiles_per_seq, 0, 0))
    return pl.pallas_call(
        _inproj_kernel,
        grid=(t // tm, IN_COLS // tn),
        in_specs=[
            pl.BlockSpec((tm, d), lambda i, n: (i, 0)),
            pl.BlockSpec((1, d), lambda i, n: (0, 0)),
            mod(), mod(),
            pl.BlockSpec((d, tn), lambda i, n: (0, n)),
            tab(), tab(), tab(),
            vec(), vec(), vec(), vec(),
        ],
        out_specs=pl.BlockSpec((tm, tn), lambda i, n: (i, n)),
        out_shape=jax.ShapeDtypeStruct((t, IN_COLS), BF16),
        scratch_shapes=[pltpu.VMEM((tm, d), BF16), pltpu.VMEM((tm, tn), F32)],
        compiler_params=_cparams(2),
        name="in_proj",
    )(x2, g, sc, sh, w, cos_t, sup_t, sdn_t, qna, kna, qnb, knb)


def _win_kernel(sink_ref, q_ref, k_ref, v_ref, o_ref, *, seq):
    g = pl.program_id(1)
    i = pl.program_id(2)
    band = 3 * WIN
    rows = RA * WIN
    row = lax.broadcasted_iota(jnp.int32, (rows, 1), 0)
    snk = jnp.full((rows, 1), sink_ref[g * RA + RA - 1], F32)
    for r in range(RA - 2, -1, -1):
        snk = jnp.where(row < (r + 1) * WIN, sink_ref[g * RA + r], snk)
    q_in_blk = row & (WIN - 1)
    col = lax.broadcasted_iota(jnp.int32, (1, band), 1)
    for j in range(ATT_TQ // WIN):
        blk = i * (ATT_TQ // WIN) + j
        start = pl.multiple_of(jnp.clip((blk - 1) * WIN, 0, seq - band), WIN)
        kw = k_ref[pl.ds(start, band), :]
        vw = v_ref[pl.ds(start, band), :]
        qs = jnp.concatenate(
            [q_ref[j * WIN:(j + 1) * WIN, r * HEAD_DIM:(r + 1) * HEAD_DIM] for r in range(RA)],
            axis=0)
        s = lax.dot_general(qs, kw, (((1,), (1,)), ((), ())), preferred_element_type=F32)
        rel = (start + col) - (blk * WIN + q_in_blk)
        s = jnp.where(jnp.abs(rel) <= WIN, s, NEG_INF)
        m = jnp.maximum(jnp.max(s, axis=-1, keepdims=True), snk)
        p = jnp.exp(s - m)
        denom = jnp.sum(p, axis=-1, keepdims=True) + jnp.exp(snk - m)
        o = jnp.dot(p.astype(BF16), vw, preferred_element_type=F32) / denom
        for r in range(RA):
            o_ref[j * WIN:(j + 1) * WIN, r * HEAD_DIM:(r + 1) * HEAD_DIM] = (
                o[r * WIN:(r + 1) * WIN].astype(BF16))


def _win_call(qkvg, sink, batch, seq):
    t = qkvg.shape[0]
    tq = ATT_TQ
    per_seq = seq // tq
    gw = RA * HEAD_DIM
    return pl.pallas_call(
        functools.partial(_win_kernel, seq=seq),
        grid=(batch, GA, per_seq),
        in_specs=[
            pl.BlockSpec(memory_space=pltpu.SMEM),
            pl.BlockSpec((tq, gw), lambda b, g, i: (b * per_seq + i, g)),
            pl.BlockSpec((seq, HEAD_DIM), lambda b, g, i: (b, COL_KA + g)),
            pl.BlockSpec((seq, HEAD_DIM), lambda b, g, i: (b, COL_VA + g)),
        ],
        out_specs=pl.BlockSpec((tq, gw), lambda b, g, i: (b * per_seq + i, g)),
        out_shape=jax.ShapeDtypeStruct((t, HA * HEAD_DIM), BF16),
        compiler_params=_cparams(3),
        name="win_attn",
    )(sink, qkvg, qkvg, qkvg)


def _nbr_kernel(tab_ref, q_ref, k_ref, v_ref, o_ref, bias_ref, tz_ref, *, grid_rows):
    h = pl.program_id(0)
    b = pl.program_id(1)
    i = pl.program_id(2)
    w = GRID_W
    keys = NA_KH * w

    @pl.when((b == 0) & (i == 0))
    def _():
        qi = lax.broadcasted_iota(jnp.int32, (w, 2 * w), 0)
        lane = lax.broadcasted_iota(jnp.int32, (w, 2 * w), 1)
        kc = lane & (w - 1)
        diff = kc - qi
        c0 = jnp.clip(qi - NA_KW // 2, 0, w - NA_KW)
        valid = (kc >= c0) & (kc < c0 + NA_KW)
        base = h * (N_DR * N_DC)
        for dr in range(N_DR):
            tz = jnp.zeros((w, 2 * w), F32)
            for dc in range(N_DC):
                tz = jnp.where(diff == dc - (NA_KW - 1), tab_ref[base + dr * N_DC + dc], tz)
            tz_ref[dr] = jnp.where(valid, tz, NEG_INF)
        low = lane < w
        for delta in range(NA_KH):
            for c in range(NA_KH // 2):
                bias_ref[delta, :, c * 2 * w:(c + 1) * 2 * w] = jnp.where(
                    low, tz_ref[2 * c - delta + NA_KH - 1], tz_ref[2 * c + 1 - delta + NA_KH - 1])

    n_rows = ATT_TQ // w
    offs, scores, probs, denoms = [], [], [], []
    for rr in range(n_rows):
        r = i * n_rows + rr
        r0 = jnp.clip(r - NA_KH // 2, 0, grid_rows - NA_KH)
        off = pl.multiple_of(r0 * w, w)
        offs.append(off)
        s = lax.dot_general(q_ref[rr * w:(rr + 1) * w, :], k_ref[pl.ds(off, keys), :],
                            (((1,), (1,)), ((), ())), preferred_element_type=F32)
        scores.append(s + bias_ref[r - r0])
    for s in scores:
        p = jnp.exp(s - jnp.max(s, axis=-1, keepdims=True))
        denoms.append(jnp.sum(p, axis=-1, keepdims=True))
        probs.append(p.astype(BF16))
    for rr in range(n_rows):
        o = jnp.dot(probs[rr], v_ref[pl.ds(offs[rr], keys), :], preferred_element_type=F32)
        o_ref[rr * w:(rr + 1) * w, :] = (o / denoms[rr]).astype(BF16)


def _nbr_call(qkvg, tab_flat, batch, seq):
    t = qkvg.shape[0]
    tq = ATT_TQ
    per_seq = seq // tq
    grid_rows = seq // GRID_W
    assert grid_rows >= NA_KH
    return pl.pallas_call(
        functools.partial(_nbr_kernel, grid_rows=grid_rows),
        grid=(HB, batch, per_seq),
        in_specs=[
            pl.BlockSpec(memory_space=pltpu.SMEM),
            pl.BlockSpec((tq, HEAD_DIM), lambda h, b, i: (b * per_seq + i, COL_QB + h)),
            pl.BlockSpec((seq, HEAD_DIM), lambda h, b, i: (b, COL_KB + h)),
            pl.BlockSpec((seq, HEAD_DIM), lambda h, b, i: (b, COL_VB + h)),
        ],
        out_specs=pl.BlockSpec((tq, HEAD_DIM), lambda h, b, i: (b * per_seq + i, h)),
        out_shape=jax.ShapeDtypeStruct((t, HB * HEAD_DIM), BF16),
        scratch_shapes=[pltpu.VMEM((NA_KH, GRID_W, NA_KH * GRID_W), F32),
                        pltpu.VMEM((N_DR, GRID_W, 2 * GRID_W), F32)],
        compiler_params=_cparams(3),
        name="nbr_attn",
    )(tab_flat, qkvg, qkvg, qkvg)


def _outproj_kernel(oa_ref, ob_ref, ga_ref, gb_ref, wpa_ref, wpb_ref, wo_ref, x_ref, gt_ref,
                    o_ref, acc_ref):
    c = pl.program_id(1)

    @pl.when(c == 0)
    def _():
        acc_ref[...] = jnp.zeros_like(acc_ref)

    ya = jnp.dot(oa_ref[...], wpa_ref[...], preferred_element_type=F32)
    yb = jnp.dot(ob_ref[...], wpb_ref[...], preferred_element_type=F32)
    merged = (ga_ref[...].astype(F32) * ya + gb_ref[...].astype(F32) * yb).astype(BF16)
    acc_ref[...] += jnp.dot(merged, wo_ref[...], preferred_element_type=F32)

    @pl.when(c == pl.num_programs(1) - 1)
    def _():
        o_ref[...] = x_ref[...] + gt_ref[...] * acc_ref[...]


def _outproj_call(oa, ob, qkvg, wpa, wpb, wo, x2, gt, seq):
    t, d = x2.shape
    tm, tc = OUT_TM, OUT_TC
    ga0 = COL_GA * HEAD_DIM // tc
    gb0 = COL_GB * HEAD_DIM // tc
    assert ga0 * tc == COL_GA * HEAD_DIM and gb0 * tc == COL_GB * HEAD_DIM
    tiles_per_seq = seq // tm
    return pl.pallas_call(
        _outproj_kernel,
        grid=(t // tm, d // tc),
        in_specs=[
            pl.BlockSpec((tm, oa.shape[1]), lambda i, c: (i, 0)),
            pl.BlockSpec((tm, ob.shape[1]), lambda i, c: (i, 0)),
            pl.BlockSpec((tm, tc), lambda i, c: (i, ga0 + c)),
            pl.BlockSpec((tm, tc), lambda i, c: (i, gb0 + c)),
            pl.BlockSpec((wpa.shape[0], tc), lambda i, c: (0, c)),
            pl.BlockSpec((wpb.shape[0], tc), lambda i, c: (0, c)),
            pl.BlockSpec((tc, d), lambda i, c: (c, 0)),
            pl.BlockSpec((tm, d), lambda i, c: (i, 0)),
            pl.BlockSpec((None, 1, d), lambda i, c: (i // tiles_per_seq, 0, 0)),
        ],
        out_specs=pl.BlockSpec((tm, d), lambda i, c: (i, 0)),
        out_shape=jax.ShapeDtypeStruct((t, d), F32),
        scratch_shapes=[pltpu.VMEM((tm, d), F32)],
        compiler_params=_cparams(2),
        name="out_proj",
    )(oa, ob, qkvg, qkvg, wpa, wpb, wo, x2, gt)


def _ffn_kernel(x_ref, xp_ref, xn_ref, g_ref, sc_ref, sh_ref, gt_ref, wg_ref, wv_ref,
                cwg_ref, cwv_ref, cbg_ref, cbv_ref, wd_ref, o_ref, h_ref, acc_ref, u_ref, *,
                tiles_per_seq):
    i = pl.program_id(0)
    j = pl.program_id(1)
    tm = FFN_TM
    ext = tm + 2 * HALO

    @pl.when(j == 0)
    def _():
        gain = g_ref[...] * (1.0 + sc_ref[...])

        def modulated(x):
            return _rms(x) * gain + sh_ref[...]

        acc_ref[...] = jnp.zeros_like(acc_ref)

        pos = i % tiles_per_seq
        keep_prev = jnp.where(pos == 0, 0.0, 1.0)
        keep_next = jnp.where(pos == tiles_per_seq - 1, 0.0, 1.0)
        h_ref[0:HALO, :] = (modulated(xp_ref[...]) * keep_prev).astype(BF16)
        h_ref[HALO:HALO + tm, :] = modulated(x_ref[...]).astype(BF16)
        h_ref[HALO + tm:ext, :] = (modulated(xn_ref[...]) * keep_next).astype(BF16)

    h = h_ref[...]
    n_sub = FFN_TF // FFN_SUB

    def up(k):
        cols = slice(k * FFN_SUB, (k + 1) * FFN_SUB)
        u_ref[2 * k] = jnp.dot(h, wg_ref[:, cols], preferred_element_type=F32)
        u_ref[2 * k + 1] = jnp.dot(h, wv_ref[:, cols], preferred_element_type=F32)

    def conv(slot, cw_ref, cb_ref, cols):
        prev = u_ref[slot, HALO - 1:HALO - 1 + tm, :]
        cur = u_ref[slot, HALO:HALO + tm, :]
        nxt = u_ref[slot, HALO + 1:HALO + 1 + tm, :]
        return (prev * cw_ref[0:1, cols] + cur * cw_ref[1:2, cols]
                + nxt * cw_ref[2:3, cols] + cb_ref[:, cols])

    up(0)
    for k in range(n_sub):
        if k + 1 < n_sub:
            up(k + 1)
        cols = slice(k * FFN_SUB, (k + 1) * FFN_SUB)
        gate = conv(2 * k, cwg_ref, cbg_ref, cols)
        val = conv(2 * k + 1, cwv_ref, cbv_ref, cols)
        act = (gate * _sigmoid(gate) * val).astype(BF16)
        acc_ref[...] += jnp.dot(act, wd_ref[cols, :], preferred_element_type=F32)

    @pl.when(j == pl.num_programs(1) - 1)
    def _():
        o_ref[...] = x_ref[...] + gt_ref[...] * acc_ref[...]


def _ffn_call(x2, g, sc, sh, gt, w_up, conv_w, conv_b, w_down, seq):
    t, d = x2.shape
    tm, tf = FFN_TM, FFN_TF
    d_ff = w_down.shape[0]
    nf = d_ff // tf
    tiles_per_seq = seq // tm
    halo_per_tile = tm // HALO
    n_halo = t // HALO
    mod = lambda: pl.BlockSpec((None, 1, d), lambda i, j: (i // tiles_per_seq, 0, 0))
    return pl.pallas_call(
        functools.partial(_ffn_kernel, tiles_per_seq=tiles_per_seq),
        grid=(t // tm, nf),
        in_specs=[
            pl.BlockSpec((tm, d), lambda i, j: (i, 0)),
            pl.BlockSpec((HALO, d), lambda i, j: (jnp.maximum(i * halo_per_tile - 1, 0), 0)),
            pl.BlockSpec((HALO, d),
                         lambda i, j: (jnp.minimum((i + 1) * halo_per_tile, n_halo - 1), 0)),
            pl.BlockSpec((1, d), lambda i, j: (0, 0)),
            mod(), mod(), mod(),
            pl.BlockSpec((d, tf), lambda i, j: (0, j)),
            pl.BlockSpec((d, tf), lambda i, j: (0, j + nf)),
            pl.BlockSpec((conv_w.shape[0], tf), lambda i, j: (0, j)),
            pl.BlockSpec((conv_w.shape[0], tf), lambda i, j: (0, j + nf)),
            pl.BlockSpec((1, tf), lambda i, j: (0, j)),
            pl.BlockSpec((1, tf), lambda i, j: (0, j + nf)),
            pl.BlockSpec((tf, d), lambda i, j: (j, 0)),
        ],
        out_specs=pl.BlockSpec((tm, d), lambda i, j: (i, 0)),
        out_shape=jax.ShapeDtypeStruct((t, d), F32),
        scratch_shapes=[pltpu.VMEM((tm + 2 * HALO, d), BF16), pltpu.VMEM((tm, d), F32),
                        pltpu.VMEM((2 * tf // FFN_SUB, tm + 2 * HALO, FFN_SUB), F32)],
        compiler_params=_cparams(2),
        name="conv_ffn",
    )(x2, x2, x2, g, sc, sh, gt, w_up, w_up, conv_w, conv_w, conv_b, conv_b, w_down)


def _rope_tables(seq):
    half = ROT_DIM // 2
    pos = jnp.arange(seq, dtype=F32)
    inv = jnp.float32(ROPE_THETA) ** (-jnp.arange(0, ROT_DIM, 2, dtype=F32) / ROT_DIM)
    ang = pos[:, None] * inv[None, :]
    cos, sin = jnp.cos(ang), jnp.sin(ang)
    rest = HEAD_DIM - ROT_DIM
    cos_t = jnp.concatenate([cos, cos, jnp.ones((seq, rest), F32)], axis=1)
    sin_up = jnp.concatenate([jnp.zeros((seq, half), F32), sin, jnp.zeros((seq, rest), F32)], axis=1)
    sin_dn = jnp.concatenate([-sin, jnp.zeros((seq, HEAD_DIM - half), F32)], axis=1)
    return cos_t, sin_up, sin_dn


def kernel(x, c, ada_w, ada_b, norm_mix, norm_ffn, w_in, qn_a, kn_a, qn_b, kn_b, sink_a,
           rel_bias_b, w_proj_a, w_proj_b, w_out, w_up, conv_w, conv_b, w_down):
    batch, seq, d = x.shape
    n_layers = ada_w.shape[0]
    t = batch * seq
    x2 = x.reshape(t, d)

    c_pad = jnp.pad(c, ((0, 8 - batch % 8 if batch % 8 else 0), (0, 0)))
    mod = _ada_call(c_pad, ada_w, ada_b)[:, :batch]
    mod = mod.reshape(n_layers, batch, 6, 1, d)
    cos_t, sin_up, sin_dn = _rope_tables(seq)

    for l in range(n_layers):
        sh_a, sc_a, gt_a, sh_m, sc_m, gt_m = [mod[l, :, k] for k in range(6)]
        qkvg = _inproj_call(
            x2, norm_mix[l][None], sc_a, sh_a, w_in[l].astype(BF16), cos_t, sin_up, sin_dn,
            qn_a[l][None], kn_a[l][None], qn_b[l][None], kn_b[l][None], seq)
        oa = _win_call(qkvg, sink_a[l], batch, seq)
        ob = _nbr_call(qkvg, rel_bias_b[l].reshape(-1), batch, seq)
        x2 = _outproj_call(oa, ob, qkvg, w_proj_a[l].astype(BF16), w_proj_b[l].astype(BF16),
                           w_out[l].astype(BF16), x2, gt_a, seq)
        x2 = _ffn_call(x2, norm_ffn[l][None], sc_m, sh_m, gt_m, w_up[l].astype(BF16),
                       conv_w[l], conv_b[l][None], w_down[l].astype(BF16), seq)
    return x2.reshape(batch, seq, d)
```

```python
import functools

import jax
import jax.numpy as jnp
from jax import lax
from jax.experimental import pallas as pl
from jax.experimental.pallas import tpu as pltpu

F32 = jnp.float32
BF16 = jnp.bfloat16

D_MODEL = 2048
HEAD_DIM = 128
HA = 8
GA = 2
RA = HA // GA
HB = 8
WIN = 128
GRID_W = 64
NA_KH = 8
NA_KW = 16
ROT_DIM = HEAD_DIM // 4
ROPE_THETA = 500000.0
D_FF = 5632
EPS = 1e-6
NEG_INF = -1e30
QK_SCALE = HEAD_DIM ** -0.5
IN_COLS = HA * HEAD_DIM + 2 * GA * HEAD_DIM + 3 * HB * HEAD_DIM + 2 * D_MODEL

COL_QA = 0
COL_KA = COL_QA + HA
COL_VA = COL_KA + GA
COL_QB = COL_VA + GA
COL_KB = COL_QB + HB
COL_VB = COL_KB + HB
COL_GA = COL_VB + HB
COL_GB = COL_GA + D_MODEL // HEAD_DIM

VMEM_LIMIT = 56 * 1024 * 1024
ADA_TN = 1024
INPROJ_TM = 1024
INPROJ_TN = 512
ATT_TQ = 512
OUT_TM = 512
OUT_TC = 512
FFN_TM = 512
FFN_TF = 512
FFN_DRAIN = 2
FFN_RB = 64
HALO = 16
N_DR = 2 * NA_KH - 1
N_DC = 2 * NA_KW - 1


def _cparams(n_axes):
    return pltpu.CompilerParams(dimension_semantics=("arbitrary",) * n_axes,
                                vmem_limit_bytes=VMEM_LIMIT)


def _sigmoid(x):
    return 1.0 / (1.0 + jnp.exp(-x))


def _rms(x):
    return x * lax.rsqrt(jnp.mean(x * x, axis=-1, keepdims=True) + EPS)


def _ada_kernel(c_ref, w_ref, b_ref, o_ref):
    c = c_ref[...]
    c_act = (c * _sigmoid(c)).astype(BF16)
    o_ref[...] = jnp.dot(c_act, w_ref[...].astype(BF16), preferred_element_type=F32) + b_ref[...]


def _ada_call(c_pad, ada_w, ada_b):
    n_layers, d, six_d = ada_w.shape
    rows = c_pad.shape[0]
    return pl.pallas_call(
        _ada_kernel,
        grid=(n_layers, six_d // ADA_TN),
        in_specs=[
            pl.BlockSpec((rows, d), lambda l, n: (0, 0)),
            pl.BlockSpec((None, d, ADA_TN), lambda l, n: (l, 0, n)),
            pl.BlockSpec((None, 1, ADA_TN), lambda l, n: (l, 0, n)),
        ],
        out_specs=pl.BlockSpec((None, rows, ADA_TN), lambda l, n: (l, 0, n)),
        out_shape=jax.ShapeDtypeStruct((n_layers, rows, six_d), F32),
        compiler_params=_cparams(2),
        name="ada_mod",
    )(c_pad, ada_w, ada_b.reshape(n_layers, 1, six_d))


def _rope(a, cos, sin_up, sin_dn):
    half = ROT_DIM // 2
    return (a * cos + pltpu.roll(a, half, 1) * sin_up
            + pltpu.roll(a, HEAD_DIM - half, 1) * sin_dn)


def _inproj_kernel(x_ref, g_ref, sc_ref, sh_ref, w_ref, cos_ref, sup_ref, sdn_ref,
                   qna_ref, kna_ref, qnb_ref, knb_ref, o_ref, h_ref, acc0_ref, acc1_ref, *,
                   n_tiles):
    s = pl.program_id(0)
    n_steps = pl.num_programs(0) - 1
    heads_per_tile = INPROJ_TN // HEAD_DIM
    n_cur = jnp.minimum(s, n_steps - 1) % n_tiles
    n = jnp.maximum(s - 1, 0) % n_tiles
    slot = s % 2

    @pl.when((n_cur == 0) & (s < n_steps))
    def _():
        gain = g_ref[...] * (1.0 + sc_ref[...])
        h_ref[...] = (_rms(x_ref[...]) * gain + sh_ref[...]).astype(BF16)

    @pl.when(s == 0)
    def _():
        acc1_ref[...] = jnp.zeros_like(acc1_ref)

    def put(k, val):
        o_ref[:, k * HEAD_DIM:(k + 1) * HEAD_DIM] = val.astype(BF16)

    def rope(a):
        return _rope(a, cos_ref[...], sup_ref[...], sdn_ref[...])

    n_qa = HA // heads_per_tile
    t_kva = n_qa
    t_qb = COL_QB // heads_per_tile
    t_kb = COL_KB // heads_per_tile
    t_vb = COL_VB // heads_per_tile
    t_gate = COL_GA // heads_per_tile

    def epi_qa(head):
        for k in range(heads_per_tile):
            put(k, rope(_rms(head(k)) * qna_ref[...]) * QK_SCALE)

    def epi_kva(head):
        for k in range(GA):
            put(k, rope(_rms(head(k)) * kna_ref[...]))
        for k in range(GA, heads_per_tile):
            put(k, head(k))

    def epi_qb(head):
        for k in range(heads_per_tile):
            put(k, _rms(head(k)) * qnb_ref[...] * QK_SCALE)

    def epi_kb(head):
        for k in range(heads_per_tile):
            put(k, _rms(head(k)) * knb_ref[...])

    def epi_vb(head):
        for k in range(heads_per_tile):
            put(k, head(k))

    def epi_gate(head):
        for k in range(heads_per_tile):
            put(k, _sigmoid(head(k)))

    kinds = [
        (n < n_qa, epi_qa),
        (n == t_kva, epi_kva),
        ((n >= t_qb) & (n < t_kb), epi_qb),
        ((n >= t_kb) & (n < t_vb), epi_kb),
        ((n >= t_vb) & (n < t_gate), epi_vb),
        (n >= t_gate, epi_gate),
    ]

    for parity, (wr_ref, rd_ref) in enumerate(((acc0_ref, acc1_ref), (acc1_ref, acc0_ref))):
        for cond, epilogue in kinds:
            @pl.when(cond & (slot == parity))
            def _(wr_ref=wr_ref, rd_ref=rd_ref, epilogue=epilogue):
                epilogue(lambda k: rd_ref[:, k * HEAD_DIM:(k + 1) * HEAD_DIM])
                wr_ref[...] = jnp.dot(h_ref[...], w_ref[...], preferred_element_type=F32)


def _inproj_call(x2, g, sc, sh, w, cos_t, sup_t, sdn_t, qna, kna, qnb, knb, seq):
    t, d = x2.shape
    tm, tn = INPROJ_TM, INPROJ_TN
    assert 2 * GA * HEAD_DIM == tn and COL_QB * HEAD_DIM % tn == 0
    tiles_per_seq = seq // tm
    n_tiles = IN_COLS // tn
    n_steps = (t // tm) * n_tiles

    def cur(s):
        return jnp.minimum(s, n_steps - 1)

    def prev(s):
        return jnp.maximum(s - 1, 0)

    vec = lambda: pl.BlockSpec((1, HEAD_DIM), lambda s: (0, 0))
    tab = lambda: pl.BlockSpec((tm, HEAD_DIM), lambda s: ((prev(s) // n_tiles) % tiles_per_seq, 0))
    mod = lambda: pl.BlockSpec((None, 1, d), lambda s: (cur(s) // n_tiles // tiles_per_seq, 0, 0))
    return pl.pallas_call(
        functools.partial(_inproj_kernel, n_tiles=n_tiles),
        grid=(n_steps + 1,),
        in_specs=[
            pl.BlockSpec((tm, d), lambda s: (cur(s) // n_tiles, 0)),
            pl.BlockSpec((1, d), lambda s: (0, 0)),
            mod(), mod(),
            pl.BlockSpec((d, tn), lambda s: (0, cur(s) % n_tiles)),
            tab(), tab(), tab(),
            vec(), vec(), vec(), vec(),
        ],
        out_specs=pl.BlockSpec((tm, tn), lambda s: (prev(s) // n_tiles, prev(s) % n_tiles)),
        out_shape=jax.ShapeDtypeStruct((t, IN_COLS), BF16),
        scratch_shapes=[pltpu.VMEM((tm, d), BF16), pltpu.VMEM((tm, tn), F32),
                        pltpu.VMEM((tm, tn), F32)],
        compiler_params=_cparams(1),
        name="in_proj",
    )(x2, g, sc, sh, w, cos_t, sup_t, sdn_t, qna, kna, qnb, knb)


def _win_kernel(sink_ref, q_ref, k_ref, v_ref, o_ref, *, seq):
    g = pl.program_id(1)
    i = pl.program_id(2)
    band = 3 * WIN
    rows = RA * WIN
    row = lax.broadcasted_iota(jnp.int32, (rows, 1), 0)
    snk = jnp.full((rows, 1), sink_ref[g * RA + RA - 1], F32)
    for r in range(RA - 2, -1, -1):
        snk = jnp.where(row < (r + 1) * WIN, sink_ref[g * RA + r], snk)
    q_in_blk = row & (WIN - 1)
    col = lax.broadcasted_iota(jnp.int32, (1, band), 1)
    for j in range(ATT_TQ // WIN):
        blk = i * (ATT_TQ // WIN) + j
        start = pl.multiple_of(jnp.clip((blk - 1) * WIN, 0, seq - band), WIN)
        kw = k_ref[pl.ds(start, band), :]
        vw = v_ref[pl.ds(start, band), :]
        qs = jnp.concatenate(
            [q_ref[j * WIN:(j + 1) * WIN, r * HEAD_DIM:(r + 1) * HEAD_DIM] for r in range(RA)],
            axis=0)
        s = lax.dot_general(qs, kw, (((1,), (1,)), ((), ())), preferred_element_type=F32)
        rel = (start + col) - (blk * WIN + q_in_blk)
        s = jnp.where(jnp.abs(rel) <= WIN, s, NEG_INF)
        m = jnp.maximum(jnp.max(s, axis=-1, keepdims=True), snk)
        p = jnp.exp(s - m)
        denom = jnp.sum(p, axis=-1, keepdims=True) + jnp.exp(snk - m)
        o = jnp.dot(p.astype(BF16), vw, preferred_element_type=F32) / denom
        for r in range(RA):
            o_ref[j * WIN:(j + 1) * WIN, r * HEAD_DIM:(r + 1) * HEAD_DIM] = (
                o[r * WIN:(r + 1) * WIN].astype(BF16))


def _win_call(qkvg, sink, batch, seq):
    t = qkvg.shape[0]
    tq = ATT_TQ
    per_seq = seq // tq
    gw = RA * HEAD_DIM
    return pl.pallas_call(
        functools.partial(_win_kernel, seq=seq),
        grid=(batch, GA, per_seq),
        in_specs=[
            pl.BlockSpec(memory_space=pltpu.SMEM),
            pl.BlockSpec((tq, gw), lambda b, g, i: (b * per_seq + i, g)),
            pl.BlockSpec((seq, HEAD_DIM), lambda b, g, i: (b, COL_KA + g)),
            pl.BlockSpec((seq, HEAD_DIM), lambda b, g, i: (b, COL_VA + g)),
        ],
        out_specs=pl.BlockSpec((tq, gw), lambda b, g, i: (b * per_seq + i, g)),
        out_shape=jax.ShapeDtypeStruct((t, HA * HEAD_DIM), BF16),
        compiler_params=_cparams(3),
        name="win_attn",
    )(sink, qkvg, qkvg, qkvg)


def _nbr_kernel(tab_ref, q_ref, k_ref, v_ref, o_ref, bias_ref, tz_ref, *, grid_rows):
    h = pl.program_id(0)
    b = pl.program_id(1)
    i = pl.program_id(2)
    w = GRID_W
    keys = NA_KH * w

    @pl.when((b == 0) & (i == 0))
    def _():
        qi = lax.broadcasted_iota(jnp.int32, (w, 2 * w), 0)
        lane = lax.broadcasted_iota(jnp.int32, (w, 2 * w), 1)
        kc = lane & (w - 1)
        diff = kc - qi
        c0 = jnp.clip(qi - NA_KW // 2, 0, w - NA_KW)
        valid = (kc >= c0) & (kc < c0 + NA_KW)
        base = h * (N_DR * N_DC)
        for dr in range(N_DR):
            tz = jnp.zeros((w, 2 * w), F32)
            for dc in range(N_DC):
                tz = jnp.where(diff == dc - (NA_KW - 1), tab_ref[base + dr * N_DC + dc], tz)
            tz_ref[dr] = jnp.where(valid, tz, NEG_INF)
        low = lane < w
        for delta in range(NA_KH):
            for c in range(NA_KH // 2):
                bias_ref[delta, :, c * 2 * w:(c + 1) * 2 * w] = jnp.where(
                    low, tz_ref[2 * c - delta + NA_KH - 1], tz_ref[2 * c + 1 - delta + NA_KH - 1])

    n_rows = ATT_TQ // w
    offs, scores, probs, denoms = [], [], [], []
    for rr in range(n_rows):
        r = i * n_rows + rr
        r0 = jnp.clip(r - NA_KH // 2, 0, grid_rows - NA_KH)
        off = pl.multiple_of(r0 * w, w)
        offs.append(off)
        s = lax.dot_general(q_ref[rr * w:(rr + 1) * w, :], k_ref[pl.ds(off, keys), :],
                            (((1,), (1,)), ((), ())), preferred_element_type=F32)
        scores.append(s + bias_ref[r - r0])
    for s in scores:
        p = jnp.exp(s - jnp.max(s, axis=-1, keepdims=True))
        denoms.append(jnp.sum(p, axis=-1, keepdims=True))
        probs.append(p.astype(BF16))
    for rr in range(n_rows):
        o = jnp.dot(probs[rr], v_ref[pl.ds(offs[rr], keys), :], preferred_element_type=F32)
        o_ref[rr * w:(rr + 1) * w, :] = (o / denoms[rr]).astype(BF16)


def _nbr_call(qkvg, tab_flat, batch, seq):
    t = qkvg.shape[0]
    tq = ATT_TQ
    per_seq = seq // tq
    grid_rows = seq // GRID_W
    assert grid_rows >= NA_KH
    return pl.pallas_call(
        functools.partial(_nbr_kernel, grid_rows=grid_rows),
        grid=(HB, batch, per_seq),
        in_specs=[
            pl.BlockSpec(memory_space=pltpu.SMEM),
            pl.BlockSpec((tq, HEAD_DIM), lambda h, b, i: (b * per_seq + i, COL_QB + h)),
            pl.BlockSpec((seq, HEAD_DIM), lambda h, b, i: (b, COL_KB + h)),
            pl.BlockSpec((seq, HEAD_DIM), lambda h, b, i: (b, COL_VB + h)),
        ],
        out_specs=pl.BlockSpec((tq, HEAD_DIM), lambda h, b, i: (b * per_seq + i, h)),
        out_shape=jax.ShapeDtypeStruct((t, HB * HEAD_DIM), BF16),
        scratch_shapes=[pltpu.VMEM((NA_KH, GRID_W, NA_KH * GRID_W), F32),
                        pltpu.VMEM((N_DR, GRID_W, 2 * GRID_W), F32)],
        compiler_params=_cparams(3),
        name="nbr_attn",
    )(tab_flat, qkvg, qkvg, qkvg)


def _outproj_kernel(oa_ref, ob_ref, ga_ref, gb_ref, wpa_ref, wpb_ref, wo_ref, x_ref, gt_ref,
                    o_ref, acc_ref):
    c = pl.program_id(1)

    @pl.when(c == 0)
    def _():
        acc_ref[...] = jnp.zeros_like(acc_ref)

    ya = jnp.dot(oa_ref[...], wpa_ref[...], preferred_element_type=F32)
    yb = jnp.dot(ob_ref[...], wpb_ref[...], preferred_element_type=F32)
    merged = (ga_ref[...].astype(F32) * ya + gb_ref[...].astype(F32) * yb).astype(BF16)
    acc_ref[...] += jnp.dot(merged, wo_ref[...], preferred_element_type=F32)

    @pl.when(c == pl.num_programs(1) - 1)
    def _():
        o_ref[...] = x_ref[...] + gt_ref[...] * acc_ref[...]


def _outproj_call(oa, ob, qkvg, wpa, wpb, wo, x2, gt, seq):
    t, d = x2.shape
    tm, tc = OUT_TM, OUT_TC
    ga0 = COL_GA * HEAD_DIM // tc
    gb0 = COL_GB * HEAD_DIM // tc
    assert ga0 * tc == COL_GA * HEAD_DIM and gb0 * tc == COL_GB * HEAD_DIM
    tiles_per_seq = seq // tm
    return pl.pallas_call(
        _outproj_kernel,
        grid=(t // tm, d // tc),
        in_specs=[
            pl.BlockSpec((tm, oa.shape[1]), lambda i, c: (i, 0)),
            pl.BlockSpec((tm, ob.shape[1]), lambda i, c: (i, 0)),
            pl.BlockSpec((tm, tc), lambda i, c: (i, ga0 + c)),
            pl.BlockSpec((tm, tc), lambda i, c: (i, gb0 + c)),
            pl.BlockSpec((wpa.shape[0], tc), lambda i, c: (0, c)),
            pl.BlockSpec((wpb.shape[0], tc), lambda i, c: (0, c)),
            pl.BlockSpec((tc, d), lambda i, c: (c, 0)),
            pl.BlockSpec((tm, d), lambda i, c: (i, 0)),
            pl.BlockSpec((None, 1, d), lambda i, c: (i // tiles_per_seq, 0, 0)),
        ],
        out_specs=pl.BlockSpec((tm, d), lambda i, c: (i, 0)),
        out_shape=jax.ShapeDtypeStruct((t, d), F32),
        scratch_shapes=[pltpu.VMEM((tm, d), F32)],
        compiler_params=_cparams(2),
        name="out_proj",
    )(oa, ob, qkvg, qkvg, wpa, wpb, wo, x2, gt)


def _ffn_kernel(x_ref, xp_ref, xn_ref, xr_ref, g_ref, sc_ref, sh_ref, gt_ref, wg_ref, wv_ref,
                cwg_ref, cwv_ref, cbg_ref, cbv_ref, wd_ref, o_ref, h_ref, act0_ref, act1_ref,
                ug0_ref, uv0_ref, ug1_ref, uv1_ref, *, tiles_per_seq, nf):
    s = pl.program_id(0)
    n_steps = pl.num_programs(0) - FFN_DRAIN
    tm = FFN_TM
    ext = tm + 2 * HALO
    cur = jnp.minimum(s, n_steps - 1)
    j_cur = cur % nf
    j_down = jnp.maximum(s - FFN_DRAIN, 0) % nf
    parity_of_step = s % 2

    @pl.when((j_cur == 0) & (s < n_steps))
    def _():
        gain = g_ref[...] * (1.0 + sc_ref[...])

        def modulated(x):
            return _rms(x) * gain + sh_ref[...]

        pos = (cur // nf) % tiles_per_seq
        keep_prev = jnp.where(pos == 0, 0.0, 1.0)
        keep_next = jnp.where(pos == tiles_per_seq - 1, 0.0, 1.0)
        h_ref[0:HALO, :] = (modulated(xp_ref[...]) * keep_prev).astype(BF16)
        h_ref[HALO:HALO + tm, :] = modulated(x_ref[...]).astype(BF16)
        h_ref[HALO + tm:ext, :] = (modulated(xn_ref[...]) * keep_next).astype(BF16)

    @pl.when(s == 0)
    def _():
        ug1_ref[...] = jnp.zeros_like(ug1_ref)
        uv1_ref[...] = jnp.zeros_like(uv1_ref)
        act1_ref[...] = jnp.zeros_like(act1_ref)

    @pl.when(j_down == 0)
    def _():
        o_ref[...] = jnp.zeros_like(o_ref)

    def conv(u_ref, cw_ref, cb_ref, r0, cols):
        pad = 8
        rows = FFN_RB + 2 * pad
        blk = u_ref[HALO + r0 - pad:HALO + r0 + FFN_RB + pad, cols]
        prev = pltpu.roll(blk, 1, 0)[pad:pad + FFN_RB]
        nxt = pltpu.roll(blk, rows - 1, 0)[pad:pad + FFN_RB]
        return (prev * cw_ref[0:1, cols] + blk[pad:pad + FFN_RB] * cw_ref[1:2, cols]
                + nxt * cw_ref[2:3, cols] + cb_ref[:, cols])

    def activation(rd_g, rd_v, wr_a):
        for c0 in range(0, FFN_TF, HEAD_DIM):
            cols = slice(c0, c0 + HEAD_DIM)
            for r0 in range(0, tm, FFN_RB):
                gate = conv(rd_g, cwg_ref, cbg_ref, r0, cols)
                val = conv(rd_v, cwv_ref, cbv_ref, r0, cols)
                wr_a[r0:r0 + FFN_RB, cols] = (gate * _sigmoid(gate) * val).astype(BF16)

    buffers = ((ug0_ref, uv0_ref, act0_ref, ug1_ref, uv1_ref, act1_ref),
               (ug1_ref, uv1_ref, act1_ref, ug0_ref, uv0_ref, act0_ref))
    for parity, (wr_g, wr_v, wr_a, rd_g, rd_v, rd_a) in enumerate(buffers):
        @pl.when(parity_of_step == parity)
        def _(wr_g=wr_g, wr_v=wr_v, wr_a=wr_a, rd_g=rd_g, rd_v=rd_v, rd_a=rd_a):
            activation(rd_g, rd_v, wr_a)
            h = h_ref[...]
            wr_g[...] = jnp.dot(h, wg_ref[...], preferred_element_type=F32)
            wr_v[...] = jnp.dot(h, wv_ref[...], preferred_element_type=F32)
            o_ref[...] += jnp.dot(rd_a[...], wd_ref[...], preferred_element_type=F32)

    @pl.when((j_down == nf - 1) & (s >= FFN_DRAIN))
    def _():
        o_ref[...] = xr_ref[...] + gt_ref[...] * o_ref[...]


def _ffn_call(x2, g, sc, sh, gt, w_up, conv_w, conv_b, w_down, seq):
    t, d = x2.shape
    tm, tf = FFN_TM, FFN_TF
    d_ff = w_down.shape[0]
    nf = d_ff // tf
    tiles_per_seq = seq // tm
    halo_per_tile = tm // HALO
    n_halo = t // HALO
    n_steps = (t // tm) * nf
    ext = tm + 2 * HALO

    def cur_i(s):
        return jnp.minimum(s, n_steps - 1) // nf

    def cur_j(s):
        return jnp.minimum(s, n_steps - 1) % nf

    def act_j(s):
        return jnp.clip(s - 1, 0, n_steps - 1) % nf

    def down_i(s):
        return jnp.maximum(s - FFN_DRAIN, 0) // nf

    def down_j(s):
        return jnp.maximum(s - FFN_DRAIN, 0) % nf

    mod_cur = lambda: pl.BlockSpec((None, 1, d), lambda s: (cur_i(s) // tiles_per_seq, 0, 0))
    u_buf = lambda: pltpu.VMEM((ext, tf), F32)
    a_buf = lambda: pltpu.VMEM((tm, tf), BF16)
    return pl.pallas_call(
        functools.partial(_ffn_kernel, tiles_per_seq=tiles_per_seq, nf=nf),
        grid=(n_steps + FFN_DRAIN,),
        in_specs=[
            pl.BlockSpec((tm, d), lambda s: (cur_i(s), 0)),
            pl.BlockSpec((HALO, d), lambda s: (jnp.maximum(cur_i(s) * halo_per_tile - 1, 0), 0)),
            pl.BlockSpec((HALO, d),
                         lambda s: (jnp.minimum((cur_i(s) + 1) * halo_per_tile, n_halo - 1), 0)),
            pl.BlockSpec((tm, d), lambda s: (down_i(s), 0)),
            pl.BlockSpec((1, d), lambda s: (0, 0)),
            mod_cur(), mod_cur(),
            pl.BlockSpec((None, 1, d), lambda s: (down_i(s) // tiles_per_seq, 0, 0)),
            pl.BlockSpec((d, tf), lambda s: (0, cur_j(s))),
            pl.BlockSpec((d, tf), lambda s: (0, cur_j(s) + nf)),
            pl.BlockSpec((conv_w.shape[0], tf), lambda s: (0, act_j(s))),
            pl.BlockSpec((conv_w.shape[0], tf), lambda s: (0, act_j(s) + nf)),
            pl.BlockSpec((1, tf), lambda s: (0, act_j(s))),
            pl.BlockSpec((1, tf), lambda s: (0, act_j(s) + nf)),
            pl.BlockSpec((tf, d), lambda s: (down_j(s), 0)),
        ],
        out_specs=pl.BlockSpec((tm, d), lambda s: (down_i(s), 0)),
        out_shape=jax.ShapeDtypeStruct((t, d), F32),
        scratch_shapes=[pltpu.VMEM((ext, d), BF16), a_buf(), a_buf(),
                        u_buf(), u_buf(), u_buf(), u_buf()],
        compiler_params=_cparams(1),
        name="conv_ffn",
    )(x2, x2, x2, x2, g, sc, sh, gt, w_up, w_up, conv_w, conv_w, conv_b, conv_b, w_down)


def _rope_tables(seq):
    half = ROT_DIM // 2
    pos = jnp.arange(seq, dtype=F32)
    inv = jnp.float32(ROPE_THETA) ** (-jnp.arange(0, ROT_DIM, 2, dtype=F32) / ROT_DIM)
    ang = pos[:, None] * inv[None, :]
    cos, sin = jnp.cos(ang), jnp.sin(ang)
    rest = HEAD_DIM - ROT_DIM
    cos_t = jnp.concatenate([cos, cos, jnp.ones((seq, rest), F32)], axis=1)
    sin_up = jnp.concatenate([jnp.zeros((seq, half), F32), sin, jnp.zeros((seq, rest), F32)], axis=1)
    sin_dn = jnp.concatenate([-sin, jnp.zeros((seq, HEAD_DIM - half), F32)], axis=1)
    return cos_t, sin_up, sin_dn


def kernel(x, c, ada_w, ada_b, norm_mix, norm_ffn, w_in, qn_a, kn_a, qn_b, kn_b, sink_a,
           rel_bias_b, w_proj_a, w_proj_b, w_out, w_up, conv_w, conv_b, w_down):
    batch, seq, d = x.shape
    n_layers = ada_w.shape[0]
    t = batch * seq
    x2 = x.reshape(t, d)

    c_pad = jnp.pad(c, ((0, 8 - batch % 8 if batch % 8 else 0), (0, 0)))
    mod = _ada_call(c_pad, ada_w, ada_b)[:, :batch]
    mod = mod.reshape(n_layers, batch, 6, 1, d)
    cos_t, sin_up, sin_dn = _rope_tables(seq)

    for l in range(n_layers):
        sh_a, sc_a, gt_a, sh_m, sc_m, gt_m = [mod[l, :, k] for k in range(6)]
        qkvg = _inproj_call(
            x2, norm_mix[l][None], sc_a, sh_a, w_in[l].astype(BF16), cos_t, sin_up, sin_dn,
            qn_a[l][None], kn_a[l][None], qn_b[l][None], kn_b[l][None], seq)
        oa = _win_call(qkvg, sink_a[l], batch, seq)
        ob = _nbr_call(qkvg, rel_bias_b[l].reshape(-1), batch, seq)
        x2 = _outproj_call(oa, ob, qkvg, w_proj_a[l].astype(BF16), w_proj_b[l].astype(BF16),
                           w_out[l].astype(BF16), x2, gt_a, seq)
        x2 = _ffn_call(x2, norm_ffn[l][None], sc_m, sh_m, gt_m, w_up[l].astype(BF16),
                       conv_w[l], conv_b[l][None], w_down[l].astype(BF16), seq)
    return x2.reshape(batch, seq, d)
```

```python
import functools

import jax
import jax.numpy as jnp
from jax import lax
from jax.experimental import pallas as pl
from jax.experimental.pallas import tpu as pltpu

F32 = jnp.float32
BF16 = jnp.bfloat16

D_MODEL = 2048
HEAD_DIM = 128
HA = 8
GA = 2
RA = HA // GA
HB = 8
WIN = 128
GRID_W = 64
NA_KH = 8
NA_KW = 16
ROT_DIM = HEAD_DIM // 4
ROPE_THETA = 500000.0
D_FF = 5632
EPS = 1e-6
NEG_INF = -1e30
QK_SCALE = HEAD_DIM ** -0.5
IN_COLS = HA * HEAD_DIM + 2 * GA * HEAD_DIM + 3 * HB * HEAD_DIM + 2 * D_MODEL

COL_QA = 0
COL_KA = COL_QA + HA
COL_VA = COL_KA + GA
COL_QB = COL_VA + GA
COL_KB = COL_QB + HB
COL_VB = COL_KB + HB
COL_GA = COL_VB + HB
COL_GB = COL_GA + D_MODEL // HEAD_DIM

VMEM_LIMIT = 56 * 1024 * 1024
ADA_TN = 1024
INPROJ_TM = 1024
INPROJ_TN = 512
ATT_TQ = 512
OUT_TM = 512
OUT_TC = 512
FFN_TM = 512
FFN_TF = 512
FFN_DRAIN = 2
FFN_RB = 64
HALO = 16
N_DR = 2 * NA_KH - 1
N_DC = 2 * NA_KW - 1


def _cparams(n_axes):
    return pltpu.CompilerParams(dimension_semantics=("arbitrary",) * n_axes,
                                vmem_limit_bytes=VMEM_LIMIT)


def _sigmoid(x):
    return 1.0 / (1.0 + jnp.exp(-x))


def _rms(x):
    return x * lax.rsqrt(jnp.mean(x * x, axis=-1, keepdims=True) + EPS)


def _ada_kernel(c_ref, w_ref, b_ref, o_ref):
    c = c_ref[...]
    c_act = (c * _sigmoid(c)).astype(BF16)
    o_ref[...] = jnp.dot(c_act, w_ref[...].astype(BF16), preferred_element_type=F32) + b_ref[...]


def _ada_call(c_pad, ada_w, ada_b):
    n_layers, d, six_d = ada_w.shape
    rows = c_pad.shape[0]
    return pl.pallas_call(
        _ada_kernel,
        grid=(n_layers, six_d // ADA_TN),
        in_specs=[
            pl.BlockSpec((rows, d), lambda l, n: (0, 0)),
            pl.BlockSpec((None, d, ADA_TN), lambda l, n: (l, 0, n)),
            pl.BlockSpec((None, 1, ADA_TN), lambda l, n: (l, 0, n)),
        ],
        out_specs=pl.BlockSpec((None, rows, ADA_TN), lambda l, n: (l, 0, n)),
        out_shape=jax.ShapeDtypeStruct((n_layers, rows, six_d), F32),
        compiler_params=_cparams(2),
        name="ada_mod",
    )(c_pad, ada_w, ada_b.reshape(n_layers, 1, six_d))


def _rope(a, cos, sin_up, sin_dn):
    half = ROT_DIM // 2
    return (a * cos + pltpu.roll(a, half, 1) * sin_up
            + pltpu.roll(a, HEAD_DIM - half, 1) * sin_dn)


def _inproj_kernel(x_ref, g_ref, sc_ref, sh_ref, w_ref, cos_ref, sup_ref, sdn_ref,
                   qna_ref, kna_ref, qnb_ref, knb_ref, o_ref, h_ref, acc0_ref, acc1_ref, *,
                   n_tiles):
    s = pl.program_id(0)
    n_steps = pl.num_programs(0) - 1
    heads_per_tile = INPROJ_TN // HEAD_DIM
    n_cur = jnp.minimum(s, n_steps - 1) % n_tiles
    n = jnp.maximum(s - 1, 0) % n_tiles
    slot = s % 2

    @pl.when((n_cur == 0) & (s < n_steps))
    def _():
        gain = g_ref[...] * (1.0 + sc_ref[...])
        h_ref[...] = (_rms(x_ref[...]) * gain + sh_ref[...]).astype(BF16)

    @pl.when(s == 0)
    def _():
        acc1_ref[...] = jnp.zeros_like(acc1_ref)

    def put(k, val):
        o_ref[:, k * HEAD_DIM:(k + 1) * HEAD_DIM] = val.astype(BF16)

    def rope(a):
        return _rope(a, cos_ref[...], sup_ref[...], sdn_ref[...])

    n_qa = HA // heads_per_tile
    t_kva = n_qa
    t_qb = COL_QB // heads_per_tile
    t_kb = COL_KB // heads_per_tile
    t_vb = COL_VB // heads_per_tile
    t_gate = COL_GA // heads_per_tile

    def epi_qa(head):
        for k in range(heads_per_tile):
            put(k, rope(_rms(head(k)) * qna_ref[...]) * QK_SCALE)

    def epi_kva(head):
        for k in range(GA):
            put(k, rope(_rms(head(k)) * kna_ref[...]))
        for k in range(GA, heads_per_tile):
            put(k, head(k))

    def epi_qb(head):
        for k in range(heads_per_tile):
            put(k, _rms(head(k)) * qnb_ref[...] * QK_SCALE)

    def epi_kb(head):
        for k in range(heads_per_tile):
            put(k, _rms(head(k)) * knb_ref[...])

    def epi_vb(head):
        for k in range(heads_per_tile):
            put(k, head(k))

    def epi_gate(head):
        for k in range(heads_per_tile):
            put(k, _sigmoid(head(k)))

    kinds = [
        (n < n_qa, epi_qa),
        (n == t_kva, epi_kva),
        ((n >= t_qb) & (n < t_kb), epi_qb),
        ((n >= t_kb) & (n < t_vb), epi_kb),
        ((n >= t_vb) & (n < t_gate), epi_vb),
        (n >= t_gate, epi_gate),
    ]

    for parity, (wr_ref, rd_ref) in enumerate(((acc0_ref, acc1_ref), (acc1_ref, acc0_ref))):
        for cond, epilogue in kinds:
            @pl.when(cond & (slot == parity))
            def _(wr_ref=wr_ref, rd_ref=rd_ref, epilogue=epilogue):
                epilogue(lambda k: rd_ref[:, k * HEAD_DIM:(k + 1) * HEAD_DIM])
                wr_ref[...] = jnp.dot(h_ref[...], w_ref[...], preferred_element_type=F32)


def _inproj_call(x2, g, sc, sh, w, cos_t, sup_t, sdn_t, qna, kna, qnb, knb, seq):
    t, d = x2.shape
    tm, tn = INPROJ_TM, INPROJ_TN
    assert 2 * GA * HEAD_DIM == tn and COL_QB * HEAD_DIM % tn == 0
    tiles_per_seq = seq // tm
    n_tiles = IN_COLS // tn
    n_steps = (t // tm) * n_tiles

    def cur(s):
        return jnp.minimum(s, n_steps - 1)

    def prev(s):
        return jnp.maximum(s - 1, 0)

    vec = lambda: pl.BlockSpec((1, HEAD_DIM), lambda s: (0, 0))
    tab = lambda: pl.BlockSpec((tm, HEAD_DIM), lambda s: ((prev(s) // n_tiles) % tiles_per_seq, 0))
    mod = lambda: pl.BlockSpec((None, 1, d), lambda s: (cur(s) // n_tiles // tiles_per_seq, 0, 0))
    return pl.pallas_call(
        functools.partial(_inproj_kernel, n_tiles=n_tiles),
        grid=(n_steps + 1,),
        in_specs=[
            pl.BlockSpec((tm, d), lambda s: (cur(s) // n_tiles, 0)),
            pl.BlockSpec((1, d), lambda s: (0, 0)),
            mod(), mod(),
            pl.BlockSpec((None, d, tn), lambda s: (cur(s) % n_tiles, 0, 0)),
            tab(), tab(), tab(),
            vec(), vec(), vec(), vec(),
        ],
        out_specs=pl.BlockSpec((tm, tn), lambda s: (prev(s) // n_tiles, prev(s) % n_tiles)),
        out_shape=jax.ShapeDtypeStruct((t, IN_COLS), BF16),
        scratch_shapes=[pltpu.VMEM((tm, d), BF16), pltpu.VMEM((tm, tn), F32),
                        pltpu.VMEM((tm, tn), F32)],
        compiler_params=_cparams(1),
        name="in_proj",
    )(x2, g, sc, sh, w, cos_t, sup_t, sdn_t, qna, kna, qnb, knb)


def _win_kernel(sink_ref, q_ref, k_ref, v_ref, o_ref, *, seq):
    g = pl.program_id(1)
    i = pl.program_id(2)
    band = 3 * WIN
    rows = RA * WIN
    row = lax.broadcasted_iota(jnp.int32, (rows, 1), 0)
    snk = jnp.full((rows, 1), sink_ref[g * RA + RA - 1], F32)
    for r in range(RA - 2, -1, -1):
        snk = jnp.where(row < (r + 1) * WIN, sink_ref[g * RA + r], snk)
    q_in_blk = row & (WIN - 1)
    col = lax.broadcasted_iota(jnp.int32, (1, band), 1)
    for j in range(ATT_TQ // WIN):
        blk = i * (ATT_TQ // WIN) + j
        start = pl.multiple_of(jnp.clip((blk - 1) * WIN, 0, seq - band), WIN)
        kw = k_ref[pl.ds(start, band), :]
        vw = v_ref[pl.ds(start, band), :]
        qs = jnp.concatenate(
            [q_ref[j * WIN:(j + 1) * WIN, r * HEAD_DIM:(r + 1) * HEAD_DIM] for r in range(RA)],
            axis=0)
        s = lax.dot_general(qs, kw, (((1,), (1,)), ((), ())), preferred_element_type=F32)
        rel = (start + col) - (blk * WIN + q_in_blk)
        s = jnp.where(jnp.abs(rel) <= WIN, s, NEG_INF)
        m = jnp.maximum(jnp.max(s, axis=-1, keepdims=True), snk)
        p = jnp.exp(s - m)
        denom = jnp.sum(p, axis=-1, keepdims=True) + jnp.exp(snk - m)
        o = jnp.dot(p.astype(BF16), vw, preferred_element_type=F32) / denom
        for r in range(RA):
            o_ref[j * WIN:(j + 1) * WIN, r * HEAD_DIM:(r + 1) * HEAD_DIM] = (
                o[r * WIN:(r + 1) * WIN].astype(BF16))


def _win_call(qkvg, sink, batch, seq):
    t = qkvg.shape[0]
    tq = ATT_TQ
    per_seq = seq // tq
    gw = RA * HEAD_DIM
    return pl.pallas_call(
        functools.partial(_win_kernel, seq=seq),
        grid=(batch, GA, per_seq),
        in_specs=[
            pl.BlockSpec(memory_space=pltpu.SMEM),
            pl.BlockSpec((tq, gw), lambda b, g, i: (b * per_seq + i, g)),
            pl.BlockSpec((seq, HEAD_DIM), lambda b, g, i: (b, COL_KA + g)),
            pl.BlockSpec((seq, HEAD_DIM), lambda b, g, i: (b, COL_VA + g)),
        ],
        out_specs=pl.BlockSpec((tq, gw), lambda b, g, i: (b * per_seq + i, g)),
        out_shape=jax.ShapeDtypeStruct((t, HA * HEAD_DIM), BF16),
        compiler_params=_cparams(3),
        name="win_attn",
    )(sink, qkvg, qkvg, qkvg)


def _nbr_kernel(tab_ref, q_ref, k_ref, v_ref, o_ref, bias_ref, tz_ref, *, grid_rows):
    h = pl.program_id(0)
    b = pl.program_id(1)
    i = pl.program_id(2)
    w = GRID_W
    keys = NA_KH * w

    @pl.when((b == 0) & (i == 0))
    def _():
        qi = lax.broadcasted_iota(jnp.int32, (w, 2 * w), 0)
        lane = lax.broadcasted_iota(jnp.int32, (w, 2 * w), 1)
        kc = lane & (w - 1)
        diff = kc - qi
        c0 = jnp.clip(qi - NA_KW // 2, 0, w - NA_KW)
        valid = (kc >= c0) & (kc < c0 + NA_KW)
        base = h * (N_DR * N_DC)
        for dr in range(N_DR):
            tz = jnp.zeros((w, 2 * w), F32)
            for dc in range(N_DC):
                tz = jnp.where(diff == dc - (NA_KW - 1), tab_ref[base + dr * N_DC + dc], tz)
            tz_ref[dr] = jnp.where(valid, tz, NEG_INF)
        low = lane < w
        for delta in range(NA_KH):
            for c in range(NA_KH // 2):
                bias_ref[delta, :, c * 2 * w:(c + 1) * 2 * w] = jnp.where(
                    low, tz_ref[2 * c - delta + NA_KH - 1], tz_ref[2 * c + 1 - delta + NA_KH - 1])

    n_rows = ATT_TQ // w
    offs, scores, probs, denoms = [], [], [], []
    for rr in range(n_rows):
        r = i * n_rows + rr
        r0 = jnp.clip(r - NA_KH // 2, 0, grid_rows - NA_KH)
        off = pl.multiple_of(r0 * w, w)
        offs.append(off)
        s = lax.dot_general(q_ref[rr * w:(rr + 1) * w, :], k_ref[pl.ds(off, keys), :],
                            (((1,), (1,)), ((), ())), preferred_element_type=F32)
        scores.append(s + bias_ref[r - r0])
    for s in scores:
        p = jnp.exp(s - jnp.max(s, axis=-1, keepdims=True))
        denoms.append(jnp.sum(p, axis=-1, keepdims=True))
        probs.append(p.astype(BF16))
    for rr in range(n_rows):
        o = jnp.dot(probs[rr], v_ref[pl.ds(offs[rr], keys), :], preferred_element_type=F32)
        o_ref[rr * w:(rr + 1) * w, :] = (o / denoms[rr]).astype(BF16)


def _nbr_call(qkvg, tab_flat, batch, seq):
    t = qkvg.shape[0]
    tq = ATT_TQ
    per_seq = seq // tq
    grid_rows = seq // GRID_W
    assert grid_rows >= NA_KH
    return pl.pallas_call(
        functools.partial(_nbr_kernel, grid_rows=grid_rows),
        grid=(HB, batch, per_seq),
        in_specs=[
            pl.BlockSpec(memory_space=pltpu.SMEM),
            pl.BlockSpec((tq, HEAD_DIM), lambda h, b, i: (b * per_seq + i, COL_QB + h)),
            pl.BlockSpec((seq, HEAD_DIM), lambda h, b, i: (b, COL_KB + h)),
            pl.BlockSpec((seq, HEAD_DIM), lambda h, b, i: (b, COL_VB + h)),
        ],
        out_specs=pl.BlockSpec((tq, HEAD_DIM), lambda h, b, i: (b * per_seq + i, h)),
        out_shape=jax.ShapeDtypeStruct((t, HB * HEAD_DIM), BF16),
        scratch_shapes=[pltpu.VMEM((NA_KH, GRID_W, NA_KH * GRID_W), F32),
                        pltpu.VMEM((N_DR, GRID_W, 2 * GRID_W), F32)],
        compiler_params=_cparams(3),
        name="nbr_attn",
    )(tab_flat, qkvg, qkvg, qkvg)


def _outproj_kernel(oa_ref, ob_ref, ga_ref, gb_ref, wpa_ref, wpb_ref, wo_ref, x_ref, gt_ref,
                    o_ref, acc_ref):
    c = pl.program_id(1)

    @pl.when(c == 0)
    def _():
        acc_ref[...] = jnp.zeros_like(acc_ref)

    ya = jnp.dot(oa_ref[...], wpa_ref[...], preferred_element_type=F32)
    yb = jnp.dot(ob_ref[...], wpb_ref[...], preferred_element_type=F32)
    merged = (ga_ref[...].astype(F32) * ya + gb_ref[...].astype(F32) * yb).astype(BF16)
    acc_ref[...] += jnp.dot(merged, wo_ref[...], preferred_element_type=F32)

    @pl.when(c == pl.num_programs(1) - 1)
    def _():
        o_ref[...] = x_ref[...] + gt_ref[...] * acc_ref[...]


def _outproj_call(oa, ob, qkvg, wpa, wpb, wo, x2, gt, seq):
    t, d = x2.shape
    tm, tc = OUT_TM, OUT_TC
    ga0 = COL_GA * HEAD_DIM // tc
    gb0 = COL_GB * HEAD_DIM // tc
    assert ga0 * tc == COL_GA * HEAD_DIM and gb0 * tc == COL_GB * HEAD_DIM
    tiles_per_seq = seq // tm
    return pl.pallas_call(
        _outproj_kernel,
        grid=(t // tm, d // tc),
        in_specs=[
            pl.BlockSpec((tm, oa.shape[1]), lambda i, c: (i, 0)),
            pl.BlockSpec((tm, ob.shape[1]), lambda i, c: (i, 0)),
            pl.BlockSpec((tm, tc), lambda i, c: (i, ga0 + c)),
            pl.BlockSpec((tm, tc), lambda i, c: (i, gb0 + c)),
            pl.BlockSpec((None, wpa.shape[1], tc), lambda i, c: (c, 0, 0)),
            pl.BlockSpec((None, wpb.shape[1], tc), lambda i, c: (c, 0, 0)),
            pl.BlockSpec((tc, d), lambda i, c: (c, 0)),
            pl.BlockSpec((tm, d), lambda i, c: (i, 0)),
            pl.BlockSpec((None, 1, d), lambda i, c: (i // tiles_per_seq, 0, 0)),
        ],
        out_specs=pl.BlockSpec((tm, d), lambda i, c: (i, 0)),
        out_shape=jax.ShapeDtypeStruct((t, d), F32),
        scratch_shapes=[pltpu.VMEM((tm, d), F32)],
        compiler_params=_cparams(2),
        name="out_proj",
    )(oa, ob, qkvg, qkvg, wpa, wpb, wo, x2, gt)


def _ffn_kernel(x_ref, xp_ref, xn_ref, xr_ref, g_ref, sc_ref, sh_ref, gt_ref, wg_ref, wv_ref,
                cwg_ref, cwv_ref, cbg_ref, cbv_ref, wd_ref, o_ref, h_ref, act0_ref, act1_ref,
                ug0_ref, uv0_ref, ug1_ref, uv1_ref, *, tiles_per_seq, nf):
    s = pl.program_id(0)
    n_steps = pl.num_programs(0) - FFN_DRAIN
    tm = FFN_TM
    ext = tm + 2 * HALO
    cur = jnp.minimum(s, n_steps - 1)
    j_cur = cur % nf
    j_down = jnp.maximum(s - FFN_DRAIN, 0) % nf
    parity_of_step = s % 2

    @pl.when((j_cur == 0) & (s < n_steps))
    def _():
        gain = g_ref[...] * (1.0 + sc_ref[...])

        def modulated(x):
            return _rms(x) * gain + sh_ref[...]

        pos = (cur // nf) % tiles_per_seq
        keep_prev = jnp.where(pos == 0, 0.0, 1.0)
        keep_next = jnp.where(pos == tiles_per_seq - 1, 0.0, 1.0)
        h_ref[0:HALO, :] = (modulated(xp_ref[...]) * keep_prev).astype(BF16)
        h_ref[HALO:HALO + tm, :] = modulated(x_ref[...]).astype(BF16)
        h_ref[HALO + tm:ext, :] = (modulated(xn_ref[...]) * keep_next).astype(BF16)

    @pl.when(s == 0)
    def _():
        ug1_ref[...] = jnp.zeros_like(ug1_ref)
        uv1_ref[...] = jnp.zeros_like(uv1_ref)
        act1_ref[...] = jnp.zeros_like(act1_ref)

    @pl.when(j_down == 0)
    def _():
        o_ref[...] = jnp.zeros_like(o_ref)

    def conv(u_ref, cw_ref, cb_ref, r0, cols):
        pad = 8
        rows = FFN_RB + 2 * pad
        blk = u_ref[HALO + r0 - pad:HALO + r0 + FFN_RB + pad, cols]
        prev = pltpu.roll(blk, 1, 0)[pad:pad + FFN_RB]
        nxt = pltpu.roll(blk, rows - 1, 0)[pad:pad + FFN_RB]
        return (prev * cw_ref[0:1, cols] + blk[pad:pad + FFN_RB] * cw_ref[1:2, cols]
                + nxt * cw_ref[2:3, cols] + cb_ref[:, cols])

    def activation(rd_g, rd_v, wr_a):
        for c0 in range(0, FFN_TF, HEAD_DIM):
            cols = slice(c0, c0 + HEAD_DIM)
            for r0 in range(0, tm, FFN_RB):
                gate = conv(rd_g, cwg_ref, cbg_ref, r0, cols)
                val = conv(rd_v, cwv_ref, cbv_ref, r0, cols)
                wr_a[r0:r0 + FFN_RB, cols] = (gate * _sigmoid(gate) * val).astype(BF16)

    buffers = ((ug0_ref, uv0_ref, act0_ref, ug1_ref, uv1_ref, act1_ref),
               (ug1_ref, uv1_ref, act1_ref, ug0_ref, uv0_ref, act0_ref))
    for parity, (wr_g, wr_v, wr_a, rd_g, rd_v, rd_a) in enumerate(buffers):
        @pl.when(parity_of_step == parity)
        def _(wr_g=wr_g, wr_v=wr_v, wr_a=wr_a, rd_g=rd_g, rd_v=rd_v, rd_a=rd_a):
            activation(rd_g, rd_v, wr_a)
            h = h_ref[...]
            wr_g[...] = jnp.dot(h, wg_ref[...], preferred_element_type=F32)
            wr_v[...] = jnp.dot(h, wv_ref[...], preferred_element_type=F32)
            o_ref[...] += jnp.dot(rd_a[...], wd_ref[...], preferred_element_type=F32)

    @pl.when((j_down == nf - 1) & (s >= FFN_DRAIN))
    def _():
        o_ref[...] = xr_ref[...] + gt_ref[...] * o_ref[...]


def _ffn_call(x2, g, sc, sh, gt, w_up, conv_w, conv_b, w_down, seq):
    t, d = x2.shape
    tm, tf = FFN_TM, FFN_TF
    d_ff = w_down.shape[0]
    nf = d_ff // tf
    tiles_per_seq = seq // tm
    halo_per_tile = tm // HALO
    n_halo = t // HALO
    n_steps = (t // tm) * nf
    ext = tm + 2 * HALO

    def cur_i(s):
        return jnp.minimum(s, n_steps - 1) // nf

    def cur_j(s):
        return jnp.minimum(s, n_steps - 1) % nf

    def act_j(s):
        return jnp.clip(s - 1, 0, n_steps - 1) % nf

    def down_i(s):
        return jnp.maximum(s - FFN_DRAIN, 0) // nf

    def down_j(s):
        return jnp.maximum(s - FFN_DRAIN, 0) % nf

    mod_cur = lambda: pl.BlockSpec((None, 1, d), lambda s: (cur_i(s) // tiles_per_seq, 0, 0))
    u_buf = lambda: pltpu.VMEM((ext, tf), F32)
    a_buf = lambda: pltpu.VMEM((tm, tf), BF16)
    return pl.pallas_call(
        functools.partial(_ffn_kernel, tiles_per_seq=tiles_per_seq, nf=nf),
        grid=(n_steps + FFN_DRAIN,),
        in_specs=[
            pl.BlockSpec((tm, d), lambda s: (cur_i(s), 0)),
            pl.BlockSpec((HALO, d), lambda s: (jnp.maximum(cur_i(s) * halo_per_tile - 1, 0), 0)),
            pl.BlockSpec((HALO, d),
                         lambda s: (jnp.minimum((cur_i(s) + 1) * halo_per_tile, n_halo - 1), 0)),
            pl.BlockSpec((tm, d), lambda s: (down_i(s), 0)),
            pl.BlockSpec((1, d), lambda s: (0, 0)),
            mod_cur(), mod_cur(),
            pl.BlockSpec((None, 1, d), lambda s: (down_i(s) // tiles_per_seq, 0, 0)),
            pl.BlockSpec((None, d, tf), lambda s: (cur_j(s), 0, 0)),
            pl.BlockSpec((None, d, tf), lambda s: (cur_j(s) + nf, 0, 0)),
            pl.BlockSpec((conv_w.shape[0], tf), lambda s: (0, act_j(s))),
            pl.BlockSpec((conv_w.shape[0], tf), lambda s: (0, act_j(s) + nf)),
            pl.BlockSpec((1, tf), lambda s: (0, act_j(s))),
            pl.BlockSpec((1, tf), lambda s: (0, act_j(s) + nf)),
            pl.BlockSpec((tf, d), lambda s: (down_j(s), 0)),
        ],
        out_specs=pl.BlockSpec((tm, d), lambda s: (down_i(s), 0)),
        out_shape=jax.ShapeDtypeStruct((t, d), F32),
        scratch_shapes=[pltpu.VMEM((ext, d), BF16), a_buf(), a_buf(),
                        u_buf(), u_buf(), u_buf(), u_buf()],
        compiler_params=_cparams(1),
        name="conv_ffn",
    )(x2, x2, x2, x2, g, sc, sh, gt, w_up, w_up, conv_w, conv_w, conv_b, conv_b, w_down)


def _rope_tables(seq):
    half = ROT_DIM // 2
    pos = jnp.arange(seq, dtype=F32)
    inv = jnp.float32(ROPE_THETA) ** (-jnp.arange(0, ROT_DIM, 2, dtype=F32) / ROT_DIM)
    ang = pos[:, None] * inv[None, :]
    cos, sin = jnp.cos(ang), jnp.sin(ang)
    rest = HEAD_DIM - ROT_DIM
    cos_t = jnp.concatenate([cos, cos, jnp.ones((seq, rest), F32)], axis=1)
    sin_up = jnp.concatenate([jnp.zeros((seq, half), F32), sin, jnp.zeros((seq, rest), F32)], axis=1)
    sin_dn = jnp.concatenate([-sin, jnp.zeros((seq, HEAD_DIM - half), F32)], axis=1)
    return cos_t, sin_up, sin_dn


def kernel(x, c, ada_w, ada_b, norm_mix, norm_ffn, w_in, qn_a, kn_a, qn_b, kn_b, sink_a,
           rel_bias_b, w_proj_a, w_proj_b, w_out, w_up, conv_w, conv_b, w_down):
    batch, seq, d = x.shape
    n_layers = ada_w.shape[0]
    t = batch * seq
    x2 = x.reshape(t, d)

    c_pad = jnp.pad(c, ((0, 8 - batch % 8 if batch % 8 else 0), (0, 0)))
    mod = _ada_call(c_pad, ada_w, ada_b)[:, :batch]
    mod = mod.reshape(n_layers, batch, 6, 1, d)
    cos_t, sin_up, sin_dn = _rope_tables(seq)

    def col_tiles(w, width):
        k, n = w.shape
        return w.reshape(k, n // width, width).transpose(1, 0, 2).astype(BF16)

    for l in range(n_layers):
        sh_a, sc_a, gt_a, sh_m, sc_m, gt_m = [mod[l, :, k] for k in range(6)]
        qkvg = _inproj_call(
            x2, norm_mix[l][None], sc_a, sh_a, col_tiles(w_in[l], INPROJ_TN), cos_t, sin_up,
            sin_dn, qn_a[l][None], kn_a[l][None], qn_b[l][None], kn_b[l][None], seq)
        oa = _win_call(qkvg, sink_a[l], batch, seq)
        ob = _nbr_call(qkvg, rel_bias_b[l].reshape(-1), batch, seq)
        x2 = _outproj_call(oa, ob, qkvg, col_tiles(w_proj_a[l], OUT_TC),
                           col_tiles(w_proj_b[l], OUT_TC), w_out[l].astype(BF16), x2, gt_a, seq)
        x2 = _ffn_call(x2, norm_ffn[l][None], sc_m, sh_m, gt_m, col_tiles(w_up[l], FFN_TF),
                       conv_w[l], conv_b[l][None], w_down[l].astype(BF16), seq)
    return x2.reshape(batch, seq, d)
```

```python
import functools

import jax
import jax.numpy as jnp
from jax import lax
from jax.experimental import pallas as pl
from jax.experimental.pallas import tpu as pltpu

F32 = jnp.float32
BF16 = jnp.bfloat16

D_MODEL = 2048
HEAD_DIM = 128
HA = 8
GA = 2
RA = HA // GA
HB = 8
WIN = 128
GRID_W = 64
NA_KH = 8
NA_KW = 16
ROT_DIM = HEAD_DIM // 4
ROPE_THETA = 500000.0
D_FF = 5632
EPS = 1e-6
NEG_INF = -1e30
QK_SCALE = HEAD_DIM ** -0.5
IN_COLS = HA * HEAD_DIM + 2 * GA * HEAD_DIM + 3 * HB * HEAD_DIM + 2 * D_MODEL

COL_QA = 0
COL_KA = COL_QA + HA
COL_VA = COL_KA + GA
COL_QB = COL_VA + GA
COL_KB = COL_QB + HB
COL_VB = COL_KB + HB
COL_GA = COL_VB + HB
COL_GB = COL_GA + D_MODEL // HEAD_DIM

VMEM_LIMIT = 56 * 1024 * 1024
ADA_TN = 1024
INPROJ_TM = 1024
INPROJ_TN = 512
ATT_TQ = 512
NBR_TQ = 1024
OUT_TM = 1024
OUT_TC = 512
FFN_TM = 512
FFN_TF = 512
FFN_SUB = 256
HALO = 16
N_DR = 2 * NA_KH - 1
N_DC = 2 * NA_KW - 1


def _cparams(n_axes):
    return pltpu.CompilerParams(dimension_semantics=("arbitrary",) * n_axes,
                                vmem_limit_bytes=VMEM_LIMIT)


def _sigmoid(x):
    return 1.0 / (1.0 + jnp.exp(-x))


def _rms(x):
    return x * lax.rsqrt(jnp.mean(x * x, axis=-1, keepdims=True) + EPS)


def _ada_kernel(c_ref, w_ref, b_ref, o_ref):
    c = c_ref[...]
    c_act = (c * _sigmoid(c)).astype(BF16)
    o_ref[...] = jnp.dot(c_act, w_ref[...].astype(BF16), preferred_element_type=F32) + b_ref[...]


def _ada_call(c_pad, ada_w, ada_b):
    n_layers, d, six_d = ada_w.shape
    rows = c_pad.shape[0]
    return pl.pallas_call(
        _ada_kernel,
        grid=(n_layers, six_d // ADA_TN),
        in_specs=[
            pl.BlockSpec((rows, d), lambda l, n: (0, 0)),
            pl.BlockSpec((None, d, ADA_TN), lambda l, n: (l, 0, n)),
            pl.BlockSpec((None, 1, ADA_TN), lambda l, n: (l, 0, n)),
        ],
        out_specs=pl.BlockSpec((None, rows, ADA_TN), lambda l, n: (l, 0, n)),
        out_shape=jax.ShapeDtypeStruct((n_layers, rows, six_d), F32),
        compiler_params=_cparams(2),
        name="ada_mod",
    )(c_pad, ada_w, ada_b.reshape(n_layers, 1, six_d))


def _rope(a, cos, sin_up, sin_dn):
    half = ROT_DIM // 2
    return (a * cos + pltpu.roll(a, half, 1) * sin_up
            + pltpu.roll(a, HEAD_DIM - half, 1) * sin_dn)


def _inproj_kernel(x_ref, g_ref, sc_ref, sh_ref, w_ref, cos_ref, sup_ref, sdn_ref,
                   qna_ref, kna_ref, qnb_ref, knb_ref, o_ref, h_ref, acc0_ref, acc1_ref, *,
                   n_tiles):
    s = pl.program_id(0)
    n_steps = pl.num_programs(0) - 1
    heads_per_tile = INPROJ_TN // HEAD_DIM
    n_cur = jnp.minimum(s, n_steps - 1) % n_tiles
    n = jnp.maximum(s - 1, 0) % n_tiles
    slot = s % 2

    @pl.when((n_cur == 0) & (s < n_steps))
    def _():
        gain = g_ref[...] * (1.0 + sc_ref[...])
        h_ref[...] = (_rms(x_ref[...]) * gain + sh_ref[...]).astype(BF16)

    @pl.when(s == 0)
    def _():
        acc1_ref[...] = jnp.zeros_like(acc1_ref)

    def put(k, val):
        o_ref[:, k * HEAD_DIM:(k + 1) * HEAD_DIM] = val.astype(BF16)

    def rope(a):
        return _rope(a, cos_ref[...], sup_ref[...], sdn_ref[...])

    n_qa = HA // heads_per_tile
    t_kva = n_qa
    t_qb = COL_QB // heads_per_tile
    t_kb = COL_KB // heads_per_tile
    t_vb = COL_VB // heads_per_tile
    t_gate = COL_GA // heads_per_tile

    def epi_qa(head):
        for k in range(heads_per_tile):
            put(k, rope(_rms(head(k)) * qna_ref[...]))

    def epi_kva(head):
        for k in range(GA):
            put(k, rope(_rms(head(k)) * kna_ref[...]))
        for k in range(GA, heads_per_tile):
            put(k, head(k))

    def epi_qb(head):
        for k in range(heads_per_tile):
            put(k, _rms(head(k)) * qnb_ref[...])

    def epi_kb(head):
        for k in range(heads_per_tile):
            put(k, _rms(head(k)) * knb_ref[...])

    def epi_vb(head):
        for k in range(heads_per_tile):
            put(k, head(k))

    def epi_gate(head):
        for k in range(heads_per_tile):
            put(k, _sigmoid(head(k)))

    kinds = [
        (n < n_qa, epi_qa),
        (n == t_kva, epi_kva),
        ((n >= t_qb) & (n < t_kb), epi_qb),
        ((n >= t_kb) & (n < t_vb), epi_kb),
        ((n >= t_vb) & (n < t_gate), epi_vb),
        (n >= t_gate, epi_gate),
    ]

    for parity, (wr_ref, rd_ref) in enumerate(((acc0_ref, acc1_ref), (acc1_ref, acc0_ref))):
        for cond, epilogue in kinds:
            @pl.when(cond & (slot == parity))
            def _(wr_ref=wr_ref, rd_ref=rd_ref, epilogue=epilogue):
                epilogue(lambda k: rd_ref[:, k * HEAD_DIM:(k + 1) * HEAD_DIM])
                wr_ref[...] = jnp.dot(h_ref[...], w_ref[...], preferred_element_type=F32)


def _inproj_call(x2, g, sc, sh, w, layer, cos_t, sup_t, sdn_t, qna, kna, qnb, knb, seq):
    t, d = x2.shape
    tm, tn = INPROJ_TM, INPROJ_TN
    assert 2 * GA * HEAD_DIM == tn and COL_QB * HEAD_DIM % tn == 0
    tiles_per_seq = seq // tm
    n_tiles = IN_COLS // tn
    n_steps = (t // tm) * n_tiles

    def cur(s):
        return jnp.minimum(s, n_steps - 1)

    def prev(s):
        return jnp.maximum(s - 1, 0)

    vec = lambda: pl.BlockSpec((1, HEAD_DIM), lambda s: (0, 0))
    tab = lambda: pl.BlockSpec((tm, HEAD_DIM), lambda s: ((prev(s) // n_tiles) % tiles_per_seq, 0))
    mod = lambda: pl.BlockSpec((None, 1, d), lambda s: (cur(s) // n_tiles // tiles_per_seq, 0, 0))
    return pl.pallas_call(
        functools.partial(_inproj_kernel, n_tiles=n_tiles),
        grid=(n_steps + 1,),
        in_specs=[
            pl.BlockSpec((tm, d), lambda s: (cur(s) // n_tiles, 0)),
            pl.BlockSpec((1, d), lambda s: (0, 0)),
            mod(), mod(),
            pl.BlockSpec((None, d, tn), lambda s: (layer, 0, cur(s) % n_tiles)),
            tab(), tab(), tab(),
            vec(), vec(), vec(), vec(),
        ],
        out_specs=pl.BlockSpec((tm, tn), lambda s: (prev(s) // n_tiles, prev(s) % n_tiles)),
        out_shape=jax.ShapeDtypeStruct((t, IN_COLS), BF16),
        scratch_shapes=[pltpu.VMEM((tm, d), BF16), pltpu.VMEM((tm, tn), F32),
                        pltpu.VMEM((tm, tn), F32)],
        compiler_params=_cparams(1),
        name="in_proj",
    )(x2, g, sc, sh, w, cos_t, sup_t, sdn_t, qna, kna, qnb, knb)


def _win_kernel(sink_ref, q_ref, k_ref, v_ref, o_ref, mask_ref, *, seq):
    b = pl.program_id(0)
    g = pl.program_id(1)
    i = pl.program_id(2)
    band = 3 * WIN
    rows = RA * WIN
    row = lax.broadcasted_iota(jnp.int32, (rows, 1), 0)

    @pl.when((b == 0) & (g == 0) & (i == 0))
    def _():
        rel0 = lax.broadcasted_iota(jnp.int32, (1, band), 1) - (row & (WIN - 1))
        for kind in range(3):
            mask_ref[kind] = jnp.where(jnp.abs(rel0 - kind * WIN) <= WIN, 0.0, NEG_INF)

    snk = jnp.full((rows, 1), sink_ref[g * RA + RA - 1], F32)
    for r in range(RA - 2, -1, -1):
        snk = jnp.where(row < (r + 1) * WIN, sink_ref[g * RA + r], snk)
    for j in range(ATT_TQ // WIN):
        blk = i * (ATT_TQ // WIN) + j
        start = pl.multiple_of(jnp.clip((blk - 1) * WIN, 0, seq - band), WIN)
        kw = k_ref[pl.ds(start, band), :]
        vw = v_ref[pl.ds(start, band), :]
        qs = jnp.concatenate(
            [q_ref[j * WIN:(j + 1) * WIN, r * HEAD_DIM:(r + 1) * HEAD_DIM] for r in range(RA)],
            axis=0)
        s = lax.dot_general(qs, kw, (((1,), (1,)), ((), ())), preferred_element_type=F32)
        s = s + mask_ref[(blk * WIN - start) // WIN]
        m = jnp.maximum(jnp.max(s, axis=-1, keepdims=True), snk)
        p = jnp.exp(s - m)
        denom = jnp.sum(p, axis=-1, keepdims=True) + jnp.exp(snk - m)
        o = jnp.dot(p.astype(BF16), vw, preferred_element_type=F32) / denom
        for r in range(RA):
            o_ref[j * WIN:(j + 1) * WIN, r * HEAD_DIM:(r + 1) * HEAD_DIM] = (
                o[r * WIN:(r + 1) * WIN].astype(BF16))


def _win_call(qkvg, sink, batch, seq):
    t = qkvg.shape[0]
    tq = ATT_TQ
    per_seq = seq // tq
    gw = RA * HEAD_DIM
    return pl.pallas_call(
        functools.partial(_win_kernel, seq=seq),
        grid=(batch, GA, per_seq),
        in_specs=[
            pl.BlockSpec(memory_space=pltpu.SMEM),
            pl.BlockSpec((tq, gw), lambda b, g, i: (b * per_seq + i, g)),
            pl.BlockSpec((seq, HEAD_DIM), lambda b, g, i: (b, COL_KA + g)),
            pl.BlockSpec((seq, HEAD_DIM), lambda b, g, i: (b, COL_VA + g)),
        ],
        out_specs=pl.BlockSpec((tq, gw), lambda b, g, i: (b * per_seq + i, g)),
        out_shape=jax.ShapeDtypeStruct((t, HA * HEAD_DIM), BF16),
        scratch_shapes=[pltpu.VMEM((3, RA * WIN, 3 * WIN), F32)],
        compiler_params=_cparams(3),
        name="win_attn",
    )(sink, qkvg, qkvg, qkvg)


def _nbr_kernel(tab_ref, q_ref, k_ref, v_ref, o_ref, bias_ref, tz_ref, *, grid_rows):
    h = pl.program_id(0)
    b = pl.program_id(1)
    i = pl.program_id(2)
    w = GRID_W
    keys = NA_KH * w

    @pl.when((b == 0) & (i == 0))
    def _():
        qi = lax.broadcasted_iota(jnp.int32, (w, 2 * w), 0)
        lane = lax.broadcasted_iota(jnp.int32, (w, 2 * w), 1)
        kc = lane & (w - 1)
        diff = kc - qi
        c0 = jnp.clip(qi - NA_KW // 2, 0, w - NA_KW)
        valid = (kc >= c0) & (kc < c0 + NA_KW)
        base = h * (N_DR * N_DC)
        for dr in range(N_DR):
            tz = jnp.zeros((w, 2 * w), F32)
            for dc in range(N_DC):
                tz = jnp.where(diff == dc - (NA_KW - 1), tab_ref[base + dr * N_DC + dc], tz)
            tz_ref[dr] = jnp.where(valid, tz, NEG_INF)
        low = lane < w
        for delta in range(NA_KH):
            for c in range(NA_KH // 2):
                bias_ref[delta, :, c * 2 * w:(c + 1) * 2 * w] = jnp.where(
                    low, tz_ref[2 * c - delta + NA_KH - 1], tz_ref[2 * c + 1 - delta + NA_KH - 1])

    n_rows = NBR_TQ // w
    offs, scores, probs, denoms = [], [], [], []
    for rr in range(n_rows):
        r = i * n_rows + rr
        r0 = jnp.clip(r - NA_KH // 2, 0, grid_rows - NA_KH)
        off = pl.multiple_of(r0 * w, w)
        offs.append(off)
        s = lax.dot_general(q_ref[rr * w:(rr + 1) * w, :], k_ref[pl.ds(off, keys), :],
                            (((1,), (1,)), ((), ())), preferred_element_type=F32)
        scores.append(s + bias_ref[r - r0])
    for s in scores:
        p = jnp.exp(s - jnp.max(s, axis=-1, keepdims=True))
        denoms.append(jnp.sum(p, axis=-1, keepdims=True))
        probs.append(p.astype(BF16))
    for rr in range(n_rows):
        o = jnp.dot(probs[rr], v_ref[pl.ds(offs[rr], keys), :], preferred_element_type=F32)
        o_ref[rr * w:(rr + 1) * w, :] = (o / denoms[rr]).astype(BF16)


def _nbr_call(qkvg, tab_flat, batch, seq):
    t = qkvg.shape[0]
    tq = NBR_TQ
    per_seq = seq // tq
    grid_rows = seq // GRID_W
    assert grid_rows >= NA_KH
    return pl.pallas_call(
        functools.partial(_nbr_kernel, grid_rows=grid_rows),
        grid=(HB, batch, per_seq),
        in_specs=[
            pl.BlockSpec(memory_space=pltpu.SMEM),
            pl.BlockSpec((tq, HEAD_DIM), lambda h, b, i: (b * per_seq + i, COL_QB + h)),
            pl.BlockSpec((seq, HEAD_DIM), lambda h, b, i: (b, COL_KB + h)),
            pl.BlockSpec((seq, HEAD_DIM), lambda h, b, i: (b, COL_VB + h)),
        ],
        out_specs=pl.BlockSpec((tq, HEAD_DIM), lambda h, b, i: (b * per_seq + i, h)),
        out_shape=jax.ShapeDtypeStruct((t, HB * HEAD_DIM), BF16),
        scratch_shapes=[pltpu.VMEM((NA_KH, GRID_W, NA_KH * GRID_W), F32),
                        pltpu.VMEM((N_DR, GRID_W, 2 * GRID_W), F32)],
        compiler_params=_cparams(3),
        name="nbr_attn",
    )(tab_flat, qkvg, qkvg, qkvg)


def _outproj_kernel(oa_ref, ob_ref, ga_ref, gb_ref, wpa_ref, wpb_ref, wo_ref, x_ref, gt_ref,
                    o_ref):
    c = pl.program_id(1)

    @pl.when(c == 0)
    def _():
        o_ref[...] = jnp.zeros_like(o_ref)

    ya = jnp.dot(oa_ref[...], wpa_ref[...], preferred_element_type=F32)
    yb = jnp.dot(ob_ref[...], wpb_ref[...], preferred_element_type=F32)
    merged = (ga_ref[...].astype(F32) * ya + gb_ref[...].astype(F32) * yb).astype(BF16)
    o_ref[...] += jnp.dot(merged, wo_ref[...], preferred_element_type=F32)

    @pl.when(c == pl.num_programs(1) - 1)
    def _():
        o_ref[...] = x_ref[...] + gt_ref[...] * o_ref[...]


def _outproj_call(oa, ob, qkvg, wpa, wpb, wo, layer, x2, gt, seq):
    t, d = x2.shape
    tm, tc = OUT_TM, OUT_TC
    ga0 = COL_GA * HEAD_DIM // tc
    gb0 = COL_GB * HEAD_DIM // tc
    assert ga0 * tc == COL_GA * HEAD_DIM and gb0 * tc == COL_GB * HEAD_DIM
    tiles_per_seq = seq // tm
    return pl.pallas_call(
        _outproj_kernel,
        grid=(t // tm, d // tc),
        in_specs=[
            pl.BlockSpec((tm, oa.shape[1]), lambda i, c: (i, 0)),
            pl.BlockSpec((tm, ob.shape[1]), lambda i, c: (i, 0)),
            pl.BlockSpec((tm, tc), lambda i, c: (i, ga0 + c)),
            pl.BlockSpec((tm, tc), lambda i, c: (i, gb0 + c)),
            pl.BlockSpec((None, wpa.shape[1], tc), lambda i, c: (layer, 0, c)),
            pl.BlockSpec((None, wpb.shape[1], tc), lambda i, c: (layer, 0, c)),
            pl.BlockSpec((None, tc, d), lambda i, c: (layer, c, 0)),
            pl.BlockSpec((tm, d), lambda i, c: (i, 0)),
            pl.BlockSpec((None, 1, d), lambda i, c: (i // tiles_per_seq, 0, 0)),
        ],
        out_specs=pl.BlockSpec((tm, d), lambda i, c: (i, 0)),
        out_shape=jax.ShapeDtypeStruct((t, d), F32),
        compiler_params=_cparams(2),
        name="out_proj",
    )(oa, ob, qkvg, qkvg, wpa, wpb, wo, x2, gt)


def _ffn_kernel(x_ref, xp_ref, xn_ref, g_ref, sc_ref, sh_ref, gt_ref, wg_ref, wv_ref,
                cwg_ref, cwv_ref, cbg_ref, cbv_ref, wd_ref, o_ref, h_ref, acc_ref, u_ref, *,
                tiles_per_seq):
    i = pl.program_id(0)
    j = pl.program_id(1)
    tm = FFN_TM
    ext = tm + 2 * HALO

    @pl.when(j == 0)
    def _():
        gain = g_ref[...] * (1.0 + sc_ref[...])

        def modulated(x):
            return _rms(x) * gain + sh_ref[...]

        acc_ref[...] = jnp.zeros_like(acc_ref)

        pos = i % tiles_per_seq
        keep_prev = jnp.where(pos == 0, 0.0, 1.0)
        keep_next = jnp.where(pos == tiles_per_seq - 1, 0.0, 1.0)
        h_ref[0:HALO, :] = (modulated(xp_ref[...]) * keep_prev).astype(BF16)
        h_ref[HALO:HALO + tm, :] = modulated(x_ref[...]).astype(BF16)
        h_ref[HALO + tm:ext, :] = (modulated(xn_ref[...]) * keep_next).astype(BF16)

    h = h_ref[...]
    n_sub = FFN_TF // FFN_SUB

    def up(k):
        cols = slice(k * FFN_SUB, (k + 1) * FFN_SUB)
        u_ref[2 * k] = jnp.dot(h, wg_ref[:, cols], preferred_element_type=F32)
        u_ref[2 * k + 1] = jnp.dot(h, wv_ref[:, cols], preferred_element_type=F32)

    def conv(slot, cw_ref, cb_ref, cols):
        prev = u_ref[slot, HALO - 1:HALO - 1 + tm, :]
        cur = u_ref[slot, HALO:HALO + tm, :]
        nxt = u_ref[slot, HALO + 1:HALO + 1 + tm, :]
        return (prev * cw_ref[0:1, cols] + cur * cw_ref[1:2, cols]
                + nxt * cw_ref[2:3, cols] + cb_ref[:, cols])

    up(0)
    for k in range(n_sub):
        if k + 1 < n_sub:
            up(k + 1)
        cols = slice(k * FFN_SUB, (k + 1) * FFN_SUB)
        gate = conv(2 * k, cwg_ref, cbg_ref, cols)
        val = conv(2 * k + 1, cwv_ref, cbv_ref, cols)
        act = (gate * _sigmoid(gate) * val).astype(BF16)
        acc_ref[...] += jnp.dot(act, wd_ref[cols, :], preferred_element_type=F32)

    @pl.when(j == pl.num_programs(1) - 1)
    def _():
        o_ref[...] = x_ref[...] + gt_ref[...] * acc_ref[...]


def _ffn_call(x2, g, sc, sh, gt, w_up, conv_w, conv_b, w_down, layer, seq):
    t, d = x2.shape
    tm, tf = FFN_TM, FFN_TF
    d_ff = w_down.shape[1]
    n_taps = conv_w.shape[1]
    nf = d_ff // tf
    tiles_per_seq = seq // tm
    halo_per_tile = tm // HALO
    n_halo = t // HALO
    mod = lambda: pl.BlockSpec((None, 1, d), lambda i, j: (i // tiles_per_seq, 0, 0))
    return pl.pallas_call(
        functools.partial(_ffn_kernel, tiles_per_seq=tiles_per_seq),
        grid=(t // tm, nf),
        in_specs=[
            pl.BlockSpec((tm, d), lambda i, j: (i, 0)),
            pl.BlockSpec((HALO, d), lambda i, j: (jnp.maximum(i * halo_per_tile - 1, 0), 0)),
            pl.BlockSpec((HALO, d),
                         lambda i, j: (jnp.minimum((i + 1) * halo_per_tile, n_halo - 1), 0)),
            pl.BlockSpec((1, d), lambda i, j: (0, 0)),
            mod(), mod(), mod(),
            pl.BlockSpec((None, d, tf), lambda i, j: (layer, 0, j)),
            pl.BlockSpec((None, d, tf), lambda i, j: (layer, 0, j + nf)),
            pl.BlockSpec((None, n_taps, tf), lambda i, j: (layer, 0, j)),
            pl.BlockSpec((None, n_taps, tf), lambda i, j: (layer, 0, j + nf)),
            pl.BlockSpec((None, 1, tf), lambda i, j: (layer, 0, j)),
            pl.BlockSpec((None, 1, tf), lambda i, j: (layer, 0, j + nf)),
            pl.BlockSpec((None, tf, d), lambda i, j: (layer, j, 0)),
        ],
        out_specs=pl.BlockSpec((tm, d), lambda i, j: (i, 0)),
        out_shape=jax.ShapeDtypeStruct((t, d), F32),
        scratch_shapes=[pltpu.VMEM((tm + 2 * HALO, d), BF16), pltpu.VMEM((tm, d), F32),
                        pltpu.VMEM((2 * tf // FFN_SUB, tm + 2 * HALO, FFN_SUB), F32)],
        compiler_params=_cparams(2),
        name="conv_ffn",
    )(x2, x2, x2, g, sc, sh, gt, w_up, w_up, conv_w, conv_w, conv_b, conv_b, w_down)


def _rope_tables(seq):
    half = ROT_DIM // 2
    pos = jnp.arange(seq, dtype=F32)
    inv = jnp.float32(ROPE_THETA) ** (-jnp.arange(0, ROT_DIM, 2, dtype=F32) / ROT_DIM)
    ang = pos[:, None] * inv[None, :]
    cos, sin = jnp.cos(ang), jnp.sin(ang)
    rest = HEAD_DIM - ROT_DIM
    cos_t = jnp.concatenate([cos, cos, jnp.ones((seq, rest), F32)], axis=1)
    sin_up = jnp.concatenate([jnp.zeros((seq, half), F32), sin, jnp.zeros((seq, rest), F32)], axis=1)
    sin_dn = jnp.concatenate([-sin, jnp.zeros((seq, HEAD_DIM - half), F32)], axis=1)
    return cos_t, sin_up, sin_dn


def kernel(x, c, ada_w, ada_b, norm_mix, norm_ffn, w_in, qn_a, kn_a, qn_b, kn_b, sink_a,
           rel_bias_b, w_proj_a, w_proj_b, w_out, w_up, conv_w, conv_b, w_down):
    batch, seq, d = x.shape
    n_layers = ada_w.shape[0]
    t = batch * seq
    x2 = x.reshape(t, d)

    c_pad = jnp.pad(c, ((0, 8 - batch % 8 if batch % 8 else 0), (0, 0)))
    mod = _ada_call(c_pad, ada_w, ada_b)[:, :batch]
    mod = mod.reshape(n_layers, batch, 6, 1, d)
    cos_t, sin_up, sin_dn = _rope_tables(seq)

    w_in_b, w_pa_b, w_pb_b, w_out_b, w_up_b, w_down_b = [
        w.astype(BF16) for w in (w_in, w_proj_a, w_proj_b, w_out, w_up, w_down)]
    conv_b3 = conv_b.reshape(n_layers, 1, -1)

    for l in range(n_layers):
        sh_a, sc_a, gt_a, sh_m, sc_m, gt_m = [mod[l, :, k] for k in range(6)]
        qkvg = _inproj_call(
            x2, norm_mix[l][None], sc_a, sh_a, w_in_b, l, cos_t, sin_up, sin_dn,
            qn_a[l][None] * QK_SCALE, kn_a[l][None], qn_b[l][None] * QK_SCALE, kn_b[l][None], seq)
        oa = _win_call(qkvg, sink_a[l], batch, seq)
        ob = _nbr_call(qkvg, rel_bias_b[l].reshape(-1), batch, seq)
        x2 = _outproj_call(oa, ob, qkvg, w_pa_b, w_pb_b, w_out_b, l, x2, gt_a, seq)
        x2 = _ffn_call(x2, norm_ffn[l][None], sc_m, sh_m, gt_m, w_up_b, conv_w, conv_b3,
                       w_down_b, l, seq)
    return x2.reshape(batch, seq, d)
```

```python
import functools

import jax
import jax.numpy as jnp
from jax import lax
from jax.experimental import pallas as pl
from jax.experimental.pallas import tpu as pltpu

F32 = jnp.float32
BF16 = jnp.bfloat16

D_MODEL = 2048
HEAD_DIM = 128
HA = 8
GA = 2
RA = HA // GA
HB = 8
WIN = 128
GRID_W = 64
NA_KH = 8
NA_KW = 16
ROT_DIM = HEAD_DIM // 4
ROPE_THETA = 500000.0
D_FF = 5632
EPS = 1e-6
NEG_INF = -1e30
QK_SCALE = HEAD_DIM ** -0.5
IN_COLS = HA * HEAD_DIM + 2 * GA * HEAD_DIM + 3 * HB * HEAD_DIM + 2 * D_MODEL

COL_QA = 0
COL_KA = COL_QA + HA
COL_VA = COL_KA + GA
COL_QB = COL_VA + GA
COL_KB = COL_QB + HB
COL_VB = COL_KB + HB
COL_GA = COL_VB + HB
COL_GB = COL_GA + D_MODEL // HEAD_DIM

VMEM_LIMIT = 56 * 1024 * 1024
ADA_TN = 1024
INPROJ_TM = 1024
INPROJ_TN = 512
ATT_TQ = 2048
NBR_TQ = 4096
OUT_TM = 1024
OUT_TC = 512
FFN_TM = 512
FFN_TF = 512
FFN_SUB = 256
HALO = 16
N_DR = 2 * NA_KH - 1
N_DC = 2 * NA_KW - 1


def _cparams(n_axes):
    return pltpu.CompilerParams(dimension_semantics=("arbitrary",) * n_axes,
                                vmem_limit_bytes=VMEM_LIMIT)


def _sigmoid(x):
    return 1.0 / (1.0 + jnp.exp(-x))


def _rms(x):
    return x * lax.rsqrt(jnp.mean(x * x, axis=-1, keepdims=True) + EPS)


def _ada_kernel(c_ref, w_ref, b_ref, o_ref):
    c = c_ref[...]
    c_act = (c * _sigmoid(c)).astype(BF16)
    o_ref[...] = jnp.dot(c_act, w_ref[...].astype(BF16), preferred_element_type=F32) + b_ref[...]


def _ada_call(c_pad, ada_w, ada_b):
    n_layers, d, six_d = ada_w.shape
    rows = c_pad.shape[0]
    return pl.pallas_call(
        _ada_kernel,
        grid=(n_layers, six_d // ADA_TN),
        in_specs=[
            pl.BlockSpec((rows, d), lambda l, n: (0, 0)),
            pl.BlockSpec((None, d, ADA_TN), lambda l, n: (l, 0, n)),
            pl.BlockSpec((None, 1, ADA_TN), lambda l, n: (l, 0, n)),
        ],
        out_specs=pl.BlockSpec((None, rows, ADA_TN), lambda l, n: (l, 0, n)),
        out_shape=jax.ShapeDtypeStruct((n_layers, rows, six_d), F32),
        compiler_params=_cparams(2),
        name="ada_mod",
    )(c_pad, ada_w, ada_b.reshape(n_layers, 1, six_d))


def _rope(a, cos, sin_up, sin_dn):
    half = ROT_DIM // 2
    return (a * cos + pltpu.roll(a, half, 1) * sin_up
            + pltpu.roll(a, HEAD_DIM - half, 1) * sin_dn)


def _inproj_kernel(x_ref, mod_ref, w_ref, rope_ref, gains_ref, o_ref, h_ref, acc0_ref, acc1_ref, *,
                   n_tiles):
    s = pl.program_id(0)
    n_steps = pl.num_programs(0) - 1
    heads_per_tile = INPROJ_TN // HEAD_DIM
    n_cur = jnp.minimum(s, n_steps - 1) % n_tiles
    n = jnp.maximum(s - 1, 0) % n_tiles
    slot = s % 2

    @pl.when((n_cur == 0) & (s < n_steps))
    def _():
        h_ref[...] = (_rms(x_ref[...]) * mod_ref[0:1, :] + mod_ref[1:2, :]).astype(BF16)

    @pl.when(s == 0)
    def _():
        acc1_ref[...] = jnp.zeros_like(acc1_ref)

    def put(k, val):
        o_ref[:, k * HEAD_DIM:(k + 1) * HEAD_DIM] = val.astype(BF16)

    def rope(a):
        return _rope(a, rope_ref[0], rope_ref[1], rope_ref[2])

    qna_ref, kna_ref, qnb_ref, knb_ref = [gains_ref.at[k:k + 1, :] for k in range(4)]

    n_qa = HA // heads_per_tile
    t_kva = n_qa
    t_qb = COL_QB // heads_per_tile
    t_kb = COL_KB // heads_per_tile
    t_vb = COL_VB // heads_per_tile
    t_gate = COL_GA // heads_per_tile

    def epi_qa(head):
        for k in range(heads_per_tile):
            put(k, rope(_rms(head(k)) * qna_ref[...]))

    def epi_kva(head):
        for k in range(GA):
            put(k, rope(_rms(head(k)) * kna_ref[...]))
        for k in range(GA, heads_per_tile):
            put(k, head(k))

    def epi_qb(head):
        for k in range(heads_per_tile):
            put(k, _rms(head(k)) * qnb_ref[...])

    def epi_kb(head):
        for k in range(heads_per_tile):
            put(k, _rms(head(k)) * knb_ref[...])

    def epi_vb(head):
        for k in range(heads_per_tile):
            put(k, head(k))

    def epi_gate(head):
        for k in range(heads_per_tile):
            put(k, _sigmoid(head(k)))

    kinds = [
        (n < n_qa, epi_qa),
        (n == t_kva, epi_kva),
        ((n >= t_qb) & (n < t_kb), epi_qb),
        ((n >= t_kb) & (n < t_vb), epi_kb),
        ((n >= t_vb) & (n < t_gate), epi_vb),
        (n >= t_gate, epi_gate),
    ]

    for parity, (wr_ref, rd_ref) in enumerate(((acc0_ref, acc1_ref), (acc1_ref, acc0_ref))):
        for cond, epilogue in kinds:
            @pl.when(cond & (slot == parity))
            def _(wr_ref=wr_ref, rd_ref=rd_ref, epilogue=epilogue):
                epilogue(lambda k: rd_ref[:, k * HEAD_DIM:(k + 1) * HEAD_DIM])
                wr_ref[...] = jnp.dot(h_ref[...], w_ref[...], preferred_element_type=F32)


def _inproj_call(x2, mod, w, layer, rope_tabs, gains, seq):
    t, d = x2.shape
    tm, tn = INPROJ_TM, INPROJ_TN
    assert 2 * GA * HEAD_DIM == tn and COL_QB * HEAD_DIM % tn == 0
    tiles_per_seq = seq // tm
    n_tiles = IN_COLS // tn
    n_steps = (t // tm) * n_tiles

    def cur(s):
        return jnp.minimum(s, n_steps - 1)

    def prev(s):
        return jnp.maximum(s - 1, 0)

    return pl.pallas_call(
        functools.partial(_inproj_kernel, n_tiles=n_tiles),
        grid=(n_steps + 1,),
        in_specs=[
            pl.BlockSpec((tm, d), lambda s: (cur(s) // n_tiles, 0)),
            pl.BlockSpec((None, mod.shape[1], d),
                         lambda s: (cur(s) // n_tiles // tiles_per_seq, 0, 0)),
            pl.BlockSpec((None, d, tn), lambda s: (layer, 0, cur(s) % n_tiles)),
            pl.BlockSpec((rope_tabs.shape[0], tm, HEAD_DIM),
                         lambda s: (0, (prev(s) // n_tiles) % tiles_per_seq, 0)),
            pl.BlockSpec(gains.shape, lambda s: (0, 0)),
        ],
        out_specs=pl.BlockSpec((tm, tn), lambda s: (prev(s) // n_tiles, prev(s) % n_tiles)),
        out_shape=jax.ShapeDtypeStruct((t, IN_COLS), BF16),
        scratch_shapes=[pltpu.VMEM((tm, d), BF16), pltpu.VMEM((tm, tn), F32),
                        pltpu.VMEM((tm, tn), F32)],
        compiler_params=_cparams(1),
        name="in_proj",
    )(x2, mod, w, rope_tabs, gains)


def _win_kernel(sink_ref, q_ref, k_ref, v_ref, o_ref, mask_ref, *, seq):
    b = pl.program_id(0)
    g = pl.program_id(1)
    i = pl.program_id(2)
    band = 3 * WIN
    rows = RA * WIN
    row = lax.broadcasted_iota(jnp.int32, (rows, 1), 0)

    @pl.when((b == 0) & (g == 0) & (i == 0))
    def _():
        rel0 = lax.broadcasted_iota(jnp.int32, (1, band), 1) - (row & (WIN - 1))
        for kind in range(3):
            mask_ref[kind] = jnp.where(jnp.abs(rel0 - kind * WIN) <= WIN, 0.0, NEG_INF)

    snk = jnp.full((rows, 1), sink_ref[g * RA + RA - 1], F32)
    for r in range(RA - 2, -1, -1):
        snk = jnp.where(row < (r + 1) * WIN, sink_ref[g * RA + r], snk)
    for j in range(ATT_TQ // WIN):
        blk = i * (ATT_TQ // WIN) + j
        start = pl.multiple_of(jnp.clip((blk - 1) * WIN, 0, seq - band), WIN)
        kw = k_ref[pl.ds(start, band), :]
        vw = v_ref[pl.ds(start, band), :]
        qs = jnp.concatenate(
            [q_ref[j * WIN:(j + 1) * WIN, r * HEAD_DIM:(r + 1) * HEAD_DIM] for r in range(RA)],
            axis=0)
        s = lax.dot_general(qs, kw, (((1,), (1,)), ((), ())), preferred_element_type=F32)
        s = s + mask_ref[(blk * WIN - start) // WIN]
        m = jnp.maximum(jnp.max(s, axis=-1, keepdims=True), snk)
        p = jnp.exp(s - m)
        denom = jnp.sum(p, axis=-1, keepdims=True) + jnp.exp(snk - m)
        o = jnp.dot(p.astype(BF16), vw, preferred_element_type=F32) / denom
        for r in range(RA):
            o_ref[j * WIN:(j + 1) * WIN, r * HEAD_DIM:(r + 1) * HEAD_DIM] = (
                o[r * WIN:(r + 1) * WIN].astype(BF16))


def _win_call(qkvg, sink, batch, seq):
    t = qkvg.shape[0]
    tq = ATT_TQ
    per_seq = seq // tq
    gw = RA * HEAD_DIM
    return pl.pallas_call(
        functools.partial(_win_kernel, seq=seq),
        grid=(batch, GA, per_seq),
        in_specs=[
            pl.BlockSpec(memory_space=pltpu.SMEM),
            pl.BlockSpec((tq, gw), lambda b, g, i: (b * per_seq + i, g)),
            pl.BlockSpec((seq, HEAD_DIM), lambda b, g, i: (b, COL_KA + g)),
            pl.BlockSpec((seq, HEAD_DIM), lambda b, g, i: (b, COL_VA + g)),
        ],
        out_specs=pl.BlockSpec((tq, gw), lambda b, g, i: (b * per_seq + i, g)),
        out_shape=jax.ShapeDtypeStruct((t, HA * HEAD_DIM), BF16),
        scratch_shapes=[pltpu.VMEM((3, RA * WIN, 3 * WIN), F32)],
        compiler_params=_cparams(3),
        name="win_attn",
    )(sink, qkvg, qkvg, qkvg)


def _nbr_kernel(tab_ref, q_ref, k_ref, v_ref, o_ref, bias_ref, tz_ref, *, grid_rows):
    h = pl.program_id(0)
    b = pl.program_id(1)
    i = pl.program_id(2)
    w = GRID_W
    keys = NA_KH * w

    @pl.when((b == 0) & (i == 0))
    def _():
        qi = lax.broadcasted_iota(jnp.int32, (w, 2 * w), 0)
        lane = lax.broadcasted_iota(jnp.int32, (w, 2 * w), 1)
        kc = lane & (w - 1)
        diff = kc - qi
        c0 = jnp.clip(qi - NA_KW // 2, 0, w - NA_KW)
        valid = (kc >= c0) & (kc < c0 + NA_KW)
        base = h * (N_DR * N_DC)
        for dr in range(N_DR):
            tz = jnp.zeros((w, 2 * w), F32)
            for dc in range(N_DC):
                tz = jnp.where(diff == dc - (NA_KW - 1), tab_ref[base + dr * N_DC + dc], tz)
            tz_ref[dr] = jnp.where(valid, tz, NEG_INF)
        low = lane < w
        for delta in range(NA_KH):
            for c in range(NA_KH // 2):
                bias_ref[delta, :, c * 2 * w:(c + 1) * 2 * w] = jnp.where(
                    low, tz_ref[2 * c - delta + NA_KH - 1], tz_ref[2 * c + 1 - delta + NA_KH - 1])

    n_rows = NBR_TQ // w
    offs, scores, probs, denoms = [], [], [], []
    for rr in range(n_rows):
        r = i * n_rows + rr
        r0 = jnp.clip(r - NA_KH // 2, 0, grid_rows - NA_KH)
        off = pl.multiple_of(r0 * w, w)
        offs.append(off)
        s = lax.dot_general(q_ref[rr * w:(rr + 1) * w, :], k_ref[pl.ds(off, keys), :],
                            (((1,), (1,)), ((), ())), preferred_element_type=F32)
        scores.append(s + bias_ref[r - r0])
    for s in scores:
        p = jnp.exp(s - jnp.max(s, axis=-1, keepdims=True))
        denoms.append(jnp.sum(p, axis=-1, keepdims=True))
        probs.append(p.astype(BF16))
    for rr in range(n_rows):
        o = jnp.dot(probs[rr], v_ref[pl.ds(offs[rr], keys), :], preferred_element_type=F32)
        o_ref[rr * w:(rr + 1) * w, :] = (o / denoms[rr]).astype(BF16)


def _nbr_call(qkvg, tab_flat, batch, seq):
    t = qkvg.shape[0]
    tq = NBR_TQ
    per_seq = seq // tq
    grid_rows = seq // GRID_W
    assert grid_rows >= NA_KH
    return pl.pallas_call(
        functools.partial(_nbr_kernel, grid_rows=grid_rows),
        grid=(HB, batch, per_seq),
        in_specs=[
            pl.BlockSpec(memory_space=pltpu.SMEM),
            pl.BlockSpec((tq, HEAD_DIM), lambda h, b, i: (b * per_seq + i, COL_QB + h)),
            pl.BlockSpec((seq, HEAD_DIM), lambda h, b, i: (b, COL_KB + h)),
            pl.BlockSpec((seq, HEAD_DIM), lambda h, b, i: (b, COL_VB + h)),
        ],
        out_specs=pl.BlockSpec((tq, HEAD_DIM), lambda h, b, i: (b * per_seq + i, h)),
        out_shape=jax.ShapeDtypeStruct((t, HB * HEAD_DIM), BF16),
        scratch_shapes=[pltpu.VMEM((NA_KH, GRID_W, NA_KH * GRID_W), F32),
                        pltpu.VMEM((N_DR, GRID_W, 2 * GRID_W), F32)],
        compiler_params=_cparams(3),
        name="nbr_attn",
    )(tab_flat, qkvg, qkvg, qkvg)


def _outproj_kernel(oa_ref, ob_ref, ga_ref, gb_ref, wpa_ref, wpb_ref, wo_ref, x_ref, gt_ref,
                    o_ref):
    c = pl.program_id(1)

    @pl.when(c == 0)
    def _():
        o_ref[...] = jnp.zeros_like(o_ref)

    ya = jnp.dot(oa_ref[...], wpa_ref[...], preferred_element_type=F32)
    yb = jnp.dot(ob_ref[...], wpb_ref[...], preferred_element_type=F32)
    merged = (ga_ref[...].astype(F32) * ya + gb_ref[...].astype(F32) * yb).astype(BF16)
    o_ref[...] += jnp.dot(merged, wo_ref[...], preferred_element_type=F32)

    @pl.when(c == pl.num_programs(1) - 1)
    def _():
        o_ref[...] = x_ref[...] + gt_ref[...] * o_ref[...]


def _outproj_call(oa, ob, qkvg, wpa, wpb, wo, layer, x2, gt, seq):
    t, d = x2.shape
    tm, tc = OUT_TM, OUT_TC
    ga0 = COL_GA * HEAD_DIM // tc
    gb0 = COL_GB * HEAD_DIM // tc
    assert ga0 * tc == COL_GA * HEAD_DIM and gb0 * tc == COL_GB * HEAD_DIM
    tiles_per_seq = seq // tm
    return pl.pallas_call(
        _outproj_kernel,
        grid=(t // tm, d // tc),
        in_specs=[
            pl.BlockSpec((tm, oa.shape[1]), lambda i, c: (i, 0)),
            pl.BlockSpec((tm, ob.shape[1]), lambda i, c: (i, 0)),
            pl.BlockSpec((tm, tc), lambda i, c: (i, ga0 + c)),
            pl.BlockSpec((tm, tc), lambda i, c: (i, gb0 + c)),
            pl.BlockSpec((None, wpa.shape[1], tc), lambda i, c: (layer, 0, c)),
            pl.BlockSpec((None, wpb.shape[1], tc), lambda i, c: (layer, 0, c)),
            pl.BlockSpec((None, tc, d), lambda i, c: (layer, c, 0)),
            pl.BlockSpec((tm, d), lambda i, c: (i, 0)),
            pl.BlockSpec((None, 1, d), lambda i, c: (i // tiles_per_seq, 0, 0)),
        ],
        out_specs=pl.BlockSpec((tm, d), lambda i, c: (i, 0)),
        out_shape=jax.ShapeDtypeStruct((t, d), F32),
        compiler_params=_cparams(2),
        name="out_proj",
    )(oa, ob, qkvg, qkvg, wpa, wpb, wo, x2, gt)


def _ffn_kernel(x_ref, xp_ref, xn_ref, mod_ref, wg_ref, wv_ref, cg_ref, cv_ref, wd_ref,
                o_ref, h_ref, acc_ref, u_ref, *, tiles_per_seq):
    i = pl.program_id(0)
    j = pl.program_id(1)
    tm = FFN_TM
    ext = tm + 2 * HALO
    n_taps = cg_ref.shape[0] - 1

    @pl.when(j == 0)
    def _():
        def modulated(x):
            return _rms(x) * mod_ref[0:1, :] + mod_ref[1:2, :]

        acc_ref[...] = jnp.zeros_like(acc_ref)

        pos = i % tiles_per_seq
        keep_prev = jnp.where(pos == 0, 0.0, 1.0)
        keep_next = jnp.where(pos == tiles_per_seq - 1, 0.0, 1.0)
        h_ref[0:HALO, :] = (modulated(xp_ref[...]) * keep_prev).astype(BF16)
        h_ref[HALO:HALO + tm, :] = modulated(x_ref[...]).astype(BF16)
        h_ref[HALO + tm:ext, :] = (modulated(xn_ref[...]) * keep_next).astype(BF16)

    h = h_ref[...]
    n_sub = FFN_TF // FFN_SUB

    def up(k):
        cols = slice(k * FFN_SUB, (k + 1) * FFN_SUB)
        u_ref[2 * k] = jnp.dot(h, wg_ref[:, cols], preferred_element_type=F32)
        u_ref[2 * k + 1] = jnp.dot(h, wv_ref[:, cols], preferred_element_type=F32)

    def conv(slot, c_ref, cols):
        prev = u_ref[slot, HALO - 1:HALO - 1 + tm, :]
        cur = u_ref[slot, HALO:HALO + tm, :]
        nxt = u_ref[slot, HALO + 1:HALO + 1 + tm, :]
        return (prev * c_ref[0:1, cols] + cur * c_ref[1:2, cols]
                + nxt * c_ref[2:3, cols] + c_ref[n_taps:n_taps + 1, cols])

    up(0)
    for k in range(n_sub):
        if k + 1 < n_sub:
            up(k + 1)
        cols = slice(k * FFN_SUB, (k + 1) * FFN_SUB)
        gate = conv(2 * k, cg_ref, cols)
        val = conv(2 * k + 1, cv_ref, cols)
        act = (gate * _sigmoid(gate) * val).astype(BF16)
        acc_ref[...] += jnp.dot(act, wd_ref[cols, :], preferred_element_type=F32)

    @pl.when(j == pl.num_programs(1) - 1)
    def _():
        o_ref[...] = x_ref[...] + mod_ref[2:3, :] * acc_ref[...]


def _ffn_call(x2, mod, w_up, conv_p, w_down, layer, seq):
    t, d = x2.shape
    tm, tf = FFN_TM, FFN_TF
    d_ff = w_down.shape[1]
    n_rows = conv_p.shape[1]
    assert n_rows == 4
    nf = d_ff // tf
    tiles_per_seq = seq // tm
    halo_per_tile = tm // HALO
    n_halo = t // HALO
    return pl.pallas_call(
        functools.partial(_ffn_kernel, tiles_per_seq=tiles_per_seq),
        grid=(t // tm, nf),
        in_specs=[
            pl.BlockSpec((tm, d), lambda i, j: (i, 0)),
            pl.BlockSpec((HALO, d), lambda i, j: (jnp.maximum(i * halo_per_tile - 1, 0), 0)),
            pl.BlockSpec((HALO, d),
                         lambda i, j: (jnp.minimum((i + 1) * halo_per_tile, n_halo - 1), 0)),
            pl.BlockSpec((None, mod.shape[1], d), lambda i, j: (i // tiles_per_seq, 0, 0)),
            pl.BlockSpec((None, d, tf), lambda i, j: (layer, 0, j)),
            pl.BlockSpec((None, d, tf), lambda i, j: (layer, 0, j + nf)),
            pl.BlockSpec((None, n_rows, tf), lambda i, j: (layer, 0, j)),
            pl.BlockSpec((None, n_rows, tf), lambda i, j: (layer, 0, j + nf)),
            pl.BlockSpec((None, tf, d), lambda i, j: (layer, j, 0)),
        ],
        out_specs=pl.BlockSpec((tm, d), lambda i, j: (i, 0)),
        out_shape=jax.ShapeDtypeStruct((t, d), F32),
        scratch_shapes=[pltpu.VMEM((tm + 2 * HALO, d), BF16), pltpu.VMEM((tm, d), F32),
                        pltpu.VMEM((2 * tf // FFN_SUB, tm + 2 * HALO, FFN_SUB), F32)],
        compiler_params=_cparams(2),
        name="conv_ffn",
    )(x2, x2, x2, mod, w_up, w_up, conv_p, conv_p, w_down)


def _rope_tables(seq):
    half = ROT_DIM // 2
    pos = jnp.arange(seq, dtype=F32)
    inv = jnp.float32(ROPE_THETA) ** (-jnp.arange(0, ROT_DIM, 2, dtype=F32) / ROT_DIM)
    ang = pos[:, None] * inv[None, :]
    cos, sin = jnp.cos(ang), jnp.sin(ang)
    rest = HEAD_DIM - ROT_DIM
    cos_t = jnp.concatenate([cos, cos, jnp.ones((seq, rest), F32)], axis=1)
    sin_up = jnp.concatenate([jnp.zeros((seq, half), F32), sin, jnp.zeros((seq, rest), F32)], axis=1)
    sin_dn = jnp.concatenate([-sin, jnp.zeros((seq, HEAD_DIM - half), F32)], axis=1)
    return jnp.stack([cos_t, sin_up, sin_dn])


def kernel(x, c, ada_w, ada_b, norm_mix, norm_ffn, w_in, qn_a, kn_a, qn_b, kn_b, sink_a,
           rel_bias_b, w_proj_a, w_proj_b, w_out, w_up, conv_w, conv_b, w_down):
    batch, seq, d = x.shape
    n_layers = ada_w.shape[0]
    t = batch * seq
    x2 = x.reshape(t, d)

    c_pad = jnp.pad(c, ((0, 8 - batch % 8 if batch % 8 else 0), (0, 0)))
    mod = _ada_call(c_pad, ada_w, ada_b)[:, :batch]
    mod = mod.reshape(n_layers, batch, 6, d)
    rope_tabs = _rope_tables(seq)

    w_in_b, w_pa_b, w_pb_b, w_out_b, w_up_b, w_down_b = [
        w.astype(BF16) for w in (w_in, w_proj_a, w_proj_b, w_out, w_up, w_down)]
    conv_p = jnp.concatenate([conv_w, conv_b[:, None, :]], axis=1)

    for l in range(n_layers):
        sh_a, sc_a, gt_a, sh_m, sc_m, gt_m = [mod[l, :, k] for k in range(6)]
        mod_mix = jnp.stack([norm_mix[l] * (1.0 + sc_a), sh_a], axis=1)
        mod_ffn = jnp.stack([norm_ffn[l] * (1.0 + sc_m), sh_m, gt_m], axis=1)
        gains = jnp.stack([qn_a[l] * QK_SCALE, kn_a[l], qn_b[l] * QK_SCALE, kn_b[l]])
        qkvg = _inproj_call(x2, mod_mix, w_in_b, l, rope_tabs, gains, seq)
        oa = _win_call(qkvg, sink_a[l], batch, seq)
        ob = _nbr_call(qkvg, rel_bias_b[l].reshape(-1), batch, seq)
        x2 = _outproj_call(oa, ob, qkvg, w_pa_b, w_pb_b, w_out_b, l, x2, gt_a[:, None, :], seq)
        x2 = _ffn_call(x2, mod_ffn, w_up_b, conv_p, w_down_b, l, seq)
    return x2.reshape(batch, seq, d)
```

```python
import functools

import jax
import jax.numpy as jnp
from jax import lax
from jax.experimental import pallas as pl
from jax.experimental.pallas import tpu as pltpu

F32 = jnp.float32
BF16 = jnp.bfloat16

D_MODEL = 2048
HEAD_DIM = 128
HA = 8
GA = 2
RA = HA // GA
HB = 8
WIN = 128
GRID_W = 64
NA_KH = 8
NA_KW = 16
ROT_DIM = HEAD_DIM // 4
ROPE_THETA = 500000.0
D_FF = 5632
EPS = 1e-6
NEG_INF = -1e30
QK_SCALE = HEAD_DIM ** -0.5
IN_COLS = HA * HEAD_DIM + 2 * GA * HEAD_DIM + 3 * HB * HEAD_DIM + 2 * D_MODEL

COL_QA = 0
COL_KA = COL_QA + HA
COL_VA = COL_KA + GA
COL_QB = COL_VA + GA
COL_KB = COL_QB + HB
COL_VB = COL_KB + HB
COL_GA = COL_VB + HB
COL_GB = COL_GA + D_MODEL // HEAD_DIM

VMEM_LIMIT = 56 * 1024 * 1024
ADA_TN = 1024
INPROJ_TM = 1024
INPROJ_TN = 512
ATT_TQ = 2048
WIN_GROUP = 8
NBR_TQ = 4096
NBR_GROUP = 64
OUT_TM = 1024
OUT_TC = 512
FFN_TM = 512
FFN_TF = 512
HALO = 16
N_DR = 2 * NA_KH - 1
N_DC = 2 * NA_KW - 1


def _cparams(n_axes):
    return pltpu.CompilerParams(dimension_semantics=("arbitrary",) * n_axes,
                                vmem_limit_bytes=VMEM_LIMIT)


def _sigmoid(x):
    return 1.0 / (1.0 + jnp.exp(-x))


def _rms(x):
    return x * lax.rsqrt(jnp.mean(x * x, axis=-1, keepdims=True) + EPS)


def _ada_kernel(c_ref, w_ref, b_ref, o_ref):
    c = c_ref[...]
    c_act = (c * _sigmoid(c)).astype(BF16)
    o_ref[...] = jnp.dot(c_act, w_ref[...].astype(BF16), preferred_element_type=F32) + b_ref[...]


def _ada_call(c_pad, ada_w, ada_b):
    n_layers, d, six_d = ada_w.shape
    rows = c_pad.shape[0]
    return pl.pallas_call(
        _ada_kernel,
        grid=(n_layers, six_d // ADA_TN),
        in_specs=[
            pl.BlockSpec((rows, d), lambda l, n: (0, 0)),
            pl.BlockSpec((None, d, ADA_TN), lambda l, n: (l, 0, n)),
            pl.BlockSpec((None, 1, ADA_TN), lambda l, n: (l, 0, n)),
        ],
        out_specs=pl.BlockSpec((None, rows, ADA_TN), lambda l, n: (l, 0, n)),
        out_shape=jax.ShapeDtypeStruct((n_layers, rows, six_d), F32),
        compiler_params=_cparams(2),
        name="ada_mod",
    )(c_pad, ada_w, ada_b.reshape(n_layers, 1, six_d))


def _rope(a, cos, sin_up, sin_dn):
    half = ROT_DIM // 2
    return (a * cos + pltpu.roll(a, half, 1) * sin_up
            + pltpu.roll(a, HEAD_DIM - half, 1) * sin_dn)


def _inproj_kernel(x_ref, mod_ref, w_ref, rope_ref, gains_ref, o_ref, h_ref, acc0_ref, acc1_ref, *,
                   n_tiles):
    s = pl.program_id(0)
    n_steps = pl.num_programs(0) - 1
    heads_per_tile = INPROJ_TN // HEAD_DIM
    n_cur = jnp.minimum(s, n_steps - 1) % n_tiles
    n = jnp.maximum(s - 1, 0) % n_tiles
    slot = s % 2

    @pl.when((n_cur == 0) & (s < n_steps))
    def _():
        h_ref[...] = (_rms(x_ref[...]) * mod_ref[0:1, :] + mod_ref[1:2, :]).astype(BF16)

    @pl.when(s == 0)
    def _():
        acc1_ref[...] = jnp.zeros_like(acc1_ref)

    def put(k, val):
        o_ref[:, k * HEAD_DIM:(k + 1) * HEAD_DIM] = val.astype(BF16)

    def rope(a):
        return _rope(a, rope_ref[0], rope_ref[1], rope_ref[2])

    qna_ref, kna_ref, qnb_ref, knb_ref = [gains_ref.at[k:k + 1, :] for k in range(4)]

    n_qa = HA // heads_per_tile
    t_kva = n_qa
    t_qb = COL_QB // heads_per_tile
    t_kb = COL_KB // heads_per_tile
    t_vb = COL_VB // heads_per_tile
    t_gate = COL_GA // heads_per_tile

    def epi_qa(head):
        for k in range(heads_per_tile):
            put(k, rope(_rms(head(k)) * qna_ref[...]))

    def epi_kva(head):
        for k in range(GA):
            put(k, rope(_rms(head(k)) * kna_ref[...]))
        for k in range(GA, heads_per_tile):
            put(k, head(k))

    def epi_qb(head):
        for k in range(heads_per_tile):
            put(k, _rms(head(k)) * qnb_ref[...])

    def epi_kb(head):
        for k in range(heads_per_tile):
            put(k, _rms(head(k)) * knb_ref[...])

    def epi_vb(head):
        for k in range(heads_per_tile):
            put(k, head(k))

    def epi_gate(head):
        for k in range(heads_per_tile):
            put(k, _sigmoid(head(k)))

    kinds = [
        (n < n_qa, epi_qa),
        (n == t_kva, epi_kva),
        ((n >= t_qb) & (n < t_kb), epi_qb),
        ((n >= t_kb) & (n < t_vb), epi_kb),
        ((n >= t_vb) & (n < t_gate), epi_vb),
        (n >= t_gate, epi_gate),
    ]

    for parity, (wr_ref, rd_ref) in enumerate(((acc0_ref, acc1_ref), (acc1_ref, acc0_ref))):
        for cond, epilogue in kinds:
            @pl.when(cond & (slot == parity))
            def _(wr_ref=wr_ref, rd_ref=rd_ref, epilogue=epilogue):
                epilogue(lambda k: rd_ref[:, k * HEAD_DIM:(k + 1) * HEAD_DIM])
                wr_ref[...] = jnp.dot(h_ref[...], w_ref[...], preferred_element_type=F32)


def _inproj_call(x2, mod, w, layer, rope_tabs, gains, seq):
    t, d = x2.shape
    tm, tn = INPROJ_TM, INPROJ_TN
    assert 2 * GA * HEAD_DIM == tn and COL_QB * HEAD_DIM % tn == 0
    tiles_per_seq = seq // tm
    n_tiles = IN_COLS // tn
    n_steps = (t // tm) * n_tiles

    def cur(s):
        return jnp.minimum(s, n_steps - 1)

    def prev(s):
        return jnp.maximum(s - 1, 0)

    return pl.pallas_call(
        functools.partial(_inproj_kernel, n_tiles=n_tiles),
        grid=(n_steps + 1,),
        in_specs=[
            pl.BlockSpec((tm, d), lambda s: (cur(s) // n_tiles, 0)),
            pl.BlockSpec((None, mod.shape[1], d),
                         lambda s: (cur(s) // n_tiles // tiles_per_seq, 0, 0)),
            pl.BlockSpec((None, d, tn), lambda s: (layer, 0, cur(s) % n_tiles)),
            pl.BlockSpec((rope_tabs.shape[0], tm, HEAD_DIM),
                         lambda s: (0, (prev(s) // n_tiles) % tiles_per_seq, 0)),
            pl.BlockSpec(gains.shape, lambda s: (0, 0)),
        ],
        out_specs=pl.BlockSpec((tm, tn), lambda s: (prev(s) // n_tiles, prev(s) % n_tiles)),
        out_shape=jax.ShapeDtypeStruct((t, IN_COLS), BF16),
        scratch_shapes=[pltpu.VMEM((tm, d), BF16), pltpu.VMEM((tm, tn), F32),
                        pltpu.VMEM((tm, tn), F32)],
        compiler_params=_cparams(1),
        name="in_proj",
    )(x2, mod, w, rope_tabs, gains)


def _win_kernel(sink_ref, q_ref, k_ref, v_ref, o_ref, mask_ref, *, seq):
    b = pl.program_id(0)
    g = pl.program_id(1)
    i = pl.program_id(2)
    band = 3 * WIN
    rows = RA * WIN
    row = lax.broadcasted_iota(jnp.int32, (rows, 1), 0)

    @pl.when((b == 0) & (g == 0) & (i == 0))
    def _():
        rel0 = lax.broadcasted_iota(jnp.int32, (1, band), 1) - (row & (WIN - 1))
        for kind in range(3):
            mask_ref[kind] = jnp.where(jnp.abs(rel0 - kind * WIN) <= WIN, 0.0, NEG_INF)

    snk = jnp.full((rows, 1), sink_ref[g * RA + RA - 1], F32)
    for r in range(RA - 2, -1, -1):
        snk = jnp.where(row < (r + 1) * WIN, sink_ref[g * RA + r], snk)
    n_blocks = ATT_TQ // WIN
    for j0 in range(0, n_blocks, WIN_GROUP):
        group = range(j0, min(j0 + WIN_GROUP, n_blocks))
        starts, scores, probs, denoms = [], [], [], []
        for j in group:
            blk = i * n_blocks + j
            start = pl.multiple_of(jnp.clip((blk - 1) * WIN, 0, seq - band), WIN)
            starts.append(start)
            qs = jnp.concatenate(
                [q_ref[j * WIN:(j + 1) * WIN, r * HEAD_DIM:(r + 1) * HEAD_DIM] for r in range(RA)],
                axis=0)
            s = lax.dot_general(qs, k_ref[pl.ds(start, band), :], (((1,), (1,)), ((), ())),
                                preferred_element_type=F32)
            scores.append(s + mask_ref[(blk * WIN - start) // WIN])
        for s in scores:
            m = jnp.maximum(jnp.max(s, axis=-1, keepdims=True), snk)
            p = jnp.exp(s - m)
            denoms.append(jnp.sum(p, axis=-1, keepdims=True) + jnp.exp(snk - m))
            probs.append(p.astype(BF16))
        for idx, j in enumerate(group):
            o = jnp.dot(probs[idx], v_ref[pl.ds(starts[idx], band), :],
                        preferred_element_type=F32) / denoms[idx]
            for r in range(RA):
                o_ref[j * WIN:(j + 1) * WIN, r * HEAD_DIM:(r + 1) * HEAD_DIM] = (
                    o[r * WIN:(r + 1) * WIN].astype(BF16))


def _win_call(qkvg, sink, batch, seq):
    t = qkvg.shape[0]
    tq = ATT_TQ
    per_seq = seq // tq
    gw = RA * HEAD_DIM
    return pl.pallas_call(
        functools.partial(_win_kernel, seq=seq),
        grid=(batch, GA, per_seq),
        in_specs=[
            pl.BlockSpec(memory_space=pltpu.SMEM),
            pl.BlockSpec((tq, gw), lambda b, g, i: (b * per_seq + i, g)),
            pl.BlockSpec((seq, HEAD_DIM), lambda b, g, i: (b, COL_KA + g)),
            pl.BlockSpec((seq, HEAD_DIM), lambda b, g, i: (b, COL_VA + g)),
        ],
        out_specs=pl.BlockSpec((tq, gw), lambda b, g, i: (b * per_seq + i, g)),
        out_shape=jax.ShapeDtypeStruct((t, HA * HEAD_DIM), BF16),
        scratch_shapes=[pltpu.VMEM((3, RA * WIN, 3 * WIN), F32)],
        compiler_params=_cparams(3),
        name="win_attn",
    )(sink, qkvg, qkvg, qkvg)


def _nbr_kernel(tab_ref, q_ref, k_ref, v_ref, o_ref, bias_ref, tz_ref, *, grid_rows):
    h = pl.program_id(0)
    b = pl.program_id(1)
    i = pl.program_id(2)
    w = GRID_W
    keys = NA_KH * w

    @pl.when((b == 0) & (i == 0))
    def _():
        qi = lax.broadcasted_iota(jnp.int32, (w, 2 * w), 0)
        lane = lax.broadcasted_iota(jnp.int32, (w, 2 * w), 1)
        kc = lane & (w - 1)
        diff = kc - qi
        c0 = jnp.clip(qi - NA_KW // 2, 0, w - NA_KW)
        valid = (kc >= c0) & (kc < c0 + NA_KW)
        base = h * (N_DR * N_DC)
        for dr in range(N_DR):
            tz = jnp.zeros((w, 2 * w), F32)
            for dc in range(N_DC):
                tz = jnp.where(diff == dc - (NA_KW - 1), tab_ref[base + dr * N_DC + dc], tz)
            tz_ref[dr] = jnp.where(valid, tz, NEG_INF)
        low = lane < w
        for delta in range(NA_KH):
            for c in range(NA_KH // 2):
                bias_ref[delta, :, c * 2 * w:(c + 1) * 2 * w] = jnp.where(
                    low, tz_ref[2 * c - delta + NA_KH - 1], tz_ref[2 * c + 1 - delta + NA_KH - 1])

    n_rows = NBR_TQ // w
    for g0 in range(0, n_rows, NBR_GROUP):
        group = range(g0, min(g0 + NBR_GROUP, n_rows))
        offs, scores, probs, denoms = [], [], [], []
        for rr in group:
            r = i * n_rows + rr
            r0 = jnp.clip(r - NA_KH // 2, 0, grid_rows - NA_KH)
            off = pl.multiple_of(r0 * w, w)
            offs.append(off)
            s = lax.dot_general(q_ref[rr * w:(rr + 1) * w, :], k_ref[pl.ds(off, keys), :],
                                (((1,), (1,)), ((), ())), preferred_element_type=F32)
            scores.append(s + bias_ref[r - r0])
        for s in scores:
            p = jnp.exp(s - jnp.max(s, axis=-1, keepdims=True))
            denoms.append(jnp.sum(p, axis=-1, keepdims=True))
            probs.append(p.astype(BF16))
        for idx, rr in enumerate(group):
            o = jnp.dot(probs[idx], v_ref[pl.ds(offs[idx], keys), :], preferred_element_type=F32)
            o_ref[rr * w:(rr + 1) * w, :] = (o / denoms[idx]).astype(BF16)


def _nbr_call(qkvg, tab_flat, batch, seq):
    t = qkvg.shape[0]
    tq = NBR_TQ
    per_seq = seq // tq
    grid_rows = seq // GRID_W
    assert grid_rows >= NA_KH
    return pl.pallas_call(
        functools.partial(_nbr_kernel, grid_rows=grid_rows),
        grid=(HB, batch, per_seq),
        in_specs=[
            pl.BlockSpec(memory_space=pltpu.SMEM),
            pl.BlockSpec((tq, HEAD_DIM), lambda h, b, i: (b * per_seq + i, COL_QB + h)),
            pl.BlockSpec((seq, HEAD_DIM), lambda h, b, i: (b, COL_KB + h)),
            pl.BlockSpec((seq, HEAD_DIM), lambda h, b, i: (b, COL_VB + h)),
        ],
        out_specs=pl.BlockSpec((tq, HEAD_DIM), lambda h, b, i: (b * per_seq + i, h)),
        out_shape=jax.ShapeDtypeStruct((t, HB * HEAD_DIM), BF16),
        scratch_shapes=[pltpu.VMEM((NA_KH, GRID_W, NA_KH * GRID_W), F32),
                        pltpu.VMEM((N_DR, GRID_W, 2 * GRID_W), F32)],
        compiler_params=_cparams(3),
        name="nbr_attn",
    )(tab_flat, qkvg, qkvg, qkvg)


def _outproj_kernel(oa_ref, ob_ref, ga_ref, gb_ref, wpa_ref, wpb_ref, wo_ref, x_ref, gt_ref,
                    o_ref):
    c = pl.program_id(1)

    @pl.when(c == 0)
    def _():
        o_ref[...] = jnp.zeros_like(o_ref)

    ya = jnp.dot(oa_ref[...], wpa_ref[...], preferred_element_type=F32)
    yb = jnp.dot(ob_ref[...], wpb_ref[...], preferred_element_type=F32)
    merged = (ga_ref[...].astype(F32) * ya + gb_ref[...].astype(F32) * yb).astype(BF16)
    o_ref[...] += jnp.dot(merged, wo_ref[...], preferred_element_type=F32)

    @pl.when(c == pl.num_programs(1) - 1)
    def _():
        o_ref[...] = x_ref[...] + gt_ref[...] * o_ref[...]


def _outproj_call(oa, ob, qkvg, wpa, wpb, wo, layer, x2, gt, seq):
    t, d = x2.shape
    tm, tc = OUT_TM, OUT_TC
    ga0 = COL_GA * HEAD_DIM // tc
    gb0 = COL_GB * HEAD_DIM // tc
    assert ga0 * tc == COL_GA * HEAD_DIM and gb0 * tc == COL_GB * HEAD_DIM
    tiles_per_seq = seq // tm
    return pl.pallas_call(
        _outproj_kernel,
        grid=(t // tm, d // tc),
        in_specs=[
            pl.BlockSpec((tm, oa.shape[1]), lambda i, c: (i, 0)),
            pl.BlockSpec((tm, ob.shape[1]), lambda i, c: (i, 0)),
            pl.BlockSpec((tm, tc), lambda i, c: (i, ga0 + c)),
            pl.BlockSpec((tm, tc), lambda i, c: (i, gb0 + c)),
            pl.BlockSpec((None, wpa.shape[1], tc), lambda i, c: (layer, 0, c)),
            pl.BlockSpec((None, wpb.shape[1], tc), lambda i, c: (layer, 0, c)),
            pl.BlockSpec((None, tc, d), lambda i, c: (layer, c, 0)),
            pl.BlockSpec((tm, d), lambda i, c: (i, 0)),
            pl.BlockSpec((None, 1, d), lambda i, c: (i // tiles_per_seq, 0, 0)),
        ],
        out_specs=pl.BlockSpec((tm, d), lambda i, c: (i, 0)),
        out_shape=jax.ShapeDtypeStruct((t, d), F32),
        compiler_params=_cparams(2),
        name="out_proj",
    )(oa, ob, qkvg, qkvg, wpa, wpb, wo, x2, gt)


def _ffn_kernel(x_ref, xp_ref, xn_ref, mod_ref, wg_ref, wv_ref, cg_ref, cv_ref, wd_ref,
                o_ref, h_ref, acc_ref, u_ref, *, tiles_per_seq):
    i = pl.program_id(0)
    j = pl.program_id(1)
    tm = FFN_TM
    ext = tm + 2 * HALO
    n_taps = cg_ref.shape[0] - 1

    @pl.when(j == 0)
    def _():
        def modulated(x):
            return _rms(x) * mod_ref[0:1, :] + mod_ref[1:2, :]

        acc_ref[...] = jnp.zeros_like(acc_ref)

        pos = i % tiles_per_seq
        keep_prev = jnp.where(pos == 0, 0.0, 1.0)
        keep_next = jnp.where(pos == tiles_per_seq - 1, 0.0, 1.0)
        h_ref[0:HALO, :] = (modulated(xp_ref[...]) * keep_prev).astype(BF16)
        h_ref[HALO:HALO + tm, :] = modulated(x_ref[...]).astype(BF16)
        h_ref[HALO + tm:ext, :] = (modulated(xn_ref[...]) * keep_next).astype(BF16)

    h = h_ref[...]
    u_ref[0] = jnp.dot(h, wg_ref[...], preferred_element_type=F32)
    u_ref[1] = jnp.dot(h, wv_ref[...], preferred_element_type=F32)

    def conv(slot, c_ref):
        return (u_ref[slot, HALO - 1:HALO - 1 + tm, :] * c_ref[0:1, :]
                + u_ref[slot, HALO:HALO + tm, :] * c_ref[1:2, :]
                + u_ref[slot, HALO + 1:HALO + 1 + tm, :] * c_ref[2:3, :]
                + c_ref[n_taps:n_taps + 1, :])

    gate = conv(0, cg_ref)
    val = conv(1, cv_ref)
    act = (gate * _sigmoid(gate) * val).astype(BF16)
    acc_ref[...] += jnp.dot(act, wd_ref[...], preferred_element_type=F32)

    @pl.when(j == pl.num_programs(1) - 1)
    def _():
        o_ref[...] = x_ref[...] + mod_ref[2:3, :] * acc_ref[...]


def _ffn_call(x2, mod, w_up, conv_p, w_down, layer, seq):
    t, d = x2.shape
    tm, tf = FFN_TM, FFN_TF
    d_ff = w_down.shape[1]
    n_rows = conv_p.shape[1]
    assert n_rows == 4
    nf = d_ff // tf
    tiles_per_seq = seq // tm
    halo_per_tile = tm // HALO
    n_halo = t // HALO
    return pl.pallas_call(
        functools.partial(_ffn_kernel, tiles_per_seq=tiles_per_seq),
        grid=(t // tm, nf),
        in_specs=[
            pl.BlockSpec((tm, d), lambda i, j: (i, 0)),
            pl.BlockSpec((HALO, d), lambda i, j: (jnp.maximum(i * halo_per_tile - 1, 0), 0)),
            pl.BlockSpec((HALO, d),
                         lambda i, j: (jnp.minimum((i + 1) * halo_per_tile, n_halo - 1), 0)),
            pl.BlockSpec((None, mod.shape[1], d), lambda i, j: (i // tiles_per_seq, 0, 0)),
            pl.BlockSpec((None, d, tf), lambda i, j: (layer, 0, j)),
            pl.BlockSpec((None, d, tf), lambda i, j: (layer, 0, j + nf)),
            pl.BlockSpec((None, n_rows, tf), lambda i, j: (layer, 0, j)),
            pl.BlockSpec((None, n_rows, tf), lambda i, j: (layer, 0, j + nf)),
            pl.BlockSpec((None, tf, d), lambda i, j: (layer, j, 0)),
        ],
        out_specs=pl.BlockSpec((tm, d), lambda i, j: (i, 0)),
        out_shape=jax.ShapeDtypeStruct((t, d), F32),
        scratch_shapes=[pltpu.VMEM((tm + 2 * HALO, d), BF16), pltpu.VMEM((tm, d), F32),
                        pltpu.VMEM((2, tm + 2 * HALO, tf), F32)],
        compiler_params=_cparams(2),
        name="conv_ffn",
    )(x2, x2, x2, mod, w_up, w_up, conv_p, conv_p, w_down)


def _rope_tables(seq):
    half = ROT_DIM // 2
    pos = jnp.arange(seq, dtype=F32)
    inv = jnp.float32(ROPE_THETA) ** (-jnp.arange(0, ROT_DIM, 2, dtype=F32) / ROT_DIM)
    ang = pos[:, None] * inv[None, :]
    cos, sin = jnp.cos(ang), jnp.sin(ang)
    rest = HEAD_DIM - ROT_DIM
    cos_t = jnp.concatenate([cos, cos, jnp.ones((seq, rest), F32)], axis=1)
    sin_up = jnp.concatenate([jnp.zeros((seq, half), F32), sin, jnp.zeros((seq, rest), F32)], axis=1)
    sin_dn = jnp.concatenate([-sin, jnp.zeros((seq, HEAD_DIM - half), F32)], axis=1)
    return jnp.stack([cos_t, sin_up, sin_dn])


def kernel(x, c, ada_w, ada_b, norm_mix, norm_ffn, w_in, qn_a, kn_a, qn_b, kn_b, sink_a,
           rel_bias_b, w_proj_a, w_proj_b, w_out, w_up, conv_w, conv_b, w_down):
    batch, seq, d = x.shape
    n_layers = ada_w.shape[0]
    t = batch * seq
    x2 = x.reshape(t, d)

    c_pad = jnp.pad(c, ((0, 8 - batch % 8 if batch % 8 else 0), (0, 0)))
    mod = _ada_call(c_pad, ada_w, ada_b)[:, :batch]
    mod = mod.reshape(n_layers, batch, 6, d)
    rope_tabs = _rope_tables(seq)

    w_in_b, w_pa_b, w_pb_b, w_out_b, w_up_b, w_down_b = [
        w.astype(BF16) for w in (w_in, w_proj_a, w_proj_b, w_out, w_up, w_down)]
    conv_p = jnp.concatenate([conv_w, conv_b[:, None, :]], axis=1)

    for l in range(n_layers):
        sh_a, sc_a, gt_a, sh_m, sc_m, gt_m = [mod[l, :, k] for k in range(6)]
        mod_mix = jnp.stack([norm_mix[l] * (1.0 + sc_a), sh_a], axis=1)
        mod_ffn = jnp.stack([norm_ffn[l] * (1.0 + sc_m), sh_m, gt_m], axis=1)
        gains = jnp.stack([qn_a[l] * QK_SCALE, kn_a[l], qn_b[l] * QK_SCALE, kn_b[l]])
        qkvg = _inproj_call(x2, mod_mix, w_in_b, l, rope_tabs, gains, seq)
        oa = _win_call(qkvg, sink_a[l], batch, seq)
        ob = _nbr_call(qkvg, rel_bias_b[l].reshape(-1), batch, seq)
        x2 = _outproj_call(oa, ob, qkvg, w_pa_b, w_pb_b, w_out_b, l, x2, gt_a[:, None, :], seq)
        x2 = _ffn_call(x2, mod_ffn, w_up_b, conv_p, w_down_b, l, seq)
    return x2.reshape(batch, seq, d)
```

```python
import functools

import jax
import jax.numpy as jnp
from jax import lax
from jax.experimental import pallas as pl
from jax.experimental.pallas import tpu as pltpu

F32 = jnp.float32
BF16 = jnp.bfloat16

D_MODEL = 2048
HEAD_DIM = 128
HA = 8
GA = 2
RA = HA // GA
HB = 8
WIN = 128
GRID_W = 64
NA_KH = 8
NA_KW = 16
ROT_DIM = HEAD_DIM // 4
ROPE_THETA = 500000.0
D_FF = 5632
EPS = 1e-6
NEG_INF = -1e30
QK_SCALE = HEAD_DIM ** -0.5
IN_COLS = HA * HEAD_DIM + 2 * GA * HEAD_DIM + 3 * HB * HEAD_DIM + 2 * D_MODEL

COL_QA = 0
COL_KA = COL_QA + HA
COL_VA = COL_KA + GA
COL_QB = COL_VA + GA
COL_KB = COL_QB + HB
COL_VB = COL_KB + HB
COL_GA = COL_VB + HB
COL_GB = COL_GA + D_MODEL // HEAD_DIM

VMEM_LIMIT = 56 * 1024 * 1024
ADA_TN = 1024
INPROJ_TM = 1024
INPROJ_TN = 512
ATT_TQ = 2048
WIN_GROUP = 8
NBR_TQ = 4096
NBR_GROUP = 64
OUT_TM = 1024
OUT_TC = 512
FFN_TM = 512
FFN_TF = 512
HALO = 16
N_DR = 2 * NA_KH - 1
N_DC = 2 * NA_KW - 1


def _cparams(n_axes):
    return pltpu.CompilerParams(dimension_semantics=("arbitrary",) * n_axes,
                                vmem_limit_bytes=VMEM_LIMIT)


def _sigmoid(x):
    return 1.0 / (1.0 + jnp.exp(-x))


def _rms(x):
    return x * lax.rsqrt(jnp.mean(x * x, axis=-1, keepdims=True) + EPS)


def _ada_kernel(c_ref, w_ref, b_ref, o_ref):
    c = c_ref[...]
    c_act = (c * _sigmoid(c)).astype(BF16)
    o_ref[...] = jnp.dot(c_act, w_ref[...].astype(BF16), preferred_element_type=F32) + b_ref[...]


def _ada_call(c_pad, ada_w, ada_b):
    n_layers, d, six_d = ada_w.shape
    rows = c_pad.shape[0]
    return pl.pallas_call(
        _ada_kernel,
        grid=(n_layers, six_d // ADA_TN),
        in_specs=[
            pl.BlockSpec((rows, d), lambda l, n: (0, 0)),
            pl.BlockSpec((None, d, ADA_TN), lambda l, n: (l, 0, n)),
            pl.BlockSpec((None, 1, ADA_TN), lambda l, n: (l, 0, n)),
        ],
        out_specs=pl.BlockSpec((None, rows, ADA_TN), lambda l, n: (l, 0, n)),
        out_shape=jax.ShapeDtypeStruct((n_layers, rows, six_d), F32),
        compiler_params=_cparams(2),
        name="ada_mod",
    )(c_pad, ada_w, ada_b.reshape(n_layers, 1, six_d))


def _rope(a, cos, sin_up, sin_dn):
    half = ROT_DIM // 2
    return (a * cos + pltpu.roll(a, half, 1) * sin_up
            + pltpu.roll(a, HEAD_DIM - half, 1) * sin_dn)


def _inproj_kernel(x_ref, mod_ref, w_ref, rope_ref, gains_ref, o_ref, h_ref, acc0_ref, acc1_ref, *,
                   n_tiles):
    s = pl.program_id(0)
    n_steps = pl.num_programs(0) - 1
    heads_per_tile = INPROJ_TN // HEAD_DIM
    n_cur = jnp.minimum(s, n_steps - 1) % n_tiles
    n = jnp.maximum(s - 1, 0) % n_tiles
    slot = s % 2

    @pl.when((n_cur == 0) & (s < n_steps))
    def _():
        h_ref[...] = (_rms(x_ref[...]) * mod_ref[0:1, :] + mod_ref[1:2, :]).astype(BF16)

    @pl.when(s == 0)
    def _():
        acc1_ref[...] = jnp.zeros_like(acc1_ref)

    def put(k, val):
        o_ref[:, k * HEAD_DIM:(k + 1) * HEAD_DIM] = val.astype(BF16)

    def rope(a):
        return _rope(a, rope_ref[0], rope_ref[1], rope_ref[2])

    qna_ref, kna_ref, qnb_ref, knb_ref = [gains_ref.at[k:k + 1, :] for k in range(4)]

    n_qa = HA // heads_per_tile
    t_kva = n_qa
    t_qb = COL_QB // heads_per_tile
    t_kb = COL_KB // heads_per_tile
    t_vb = COL_VB // heads_per_tile
    t_gate = COL_GA // heads_per_tile

    def epi_qa(head):
        for k in range(heads_per_tile):
            put(k, rope(_rms(head(k)) * qna_ref[...]))

    def epi_kva(head):
        for k in range(GA):
            put(k, rope(_rms(head(k)) * kna_ref[...]))
        for k in range(GA, heads_per_tile):
            put(k, head(k))

    def epi_qb(head):
        for k in range(heads_per_tile):
            put(k, _rms(head(k)) * qnb_ref[...])

    def epi_kb(head):
        for k in range(heads_per_tile):
            put(k, _rms(head(k)) * knb_ref[...])

    def epi_vb(head):
        for k in range(heads_per_tile):
            put(k, head(k))

    def epi_gate(head):
        for k in range(heads_per_tile):
            put(k, _sigmoid(head(k)))

    kinds = [
        (n < n_qa, epi_qa),
        (n == t_kva, epi_kva),
        ((n >= t_qb) & (n < t_kb), epi_qb),
        ((n >= t_kb) & (n < t_vb), epi_kb),
        ((n >= t_vb) & (n < t_gate), epi_vb),
        (n >= t_gate, epi_gate),
    ]

    for parity, (wr_ref, rd_ref) in enumerate(((acc0_ref, acc1_ref), (acc1_ref, acc0_ref))):
        for cond, epilogue in kinds:
            @pl.when(cond & (slot == parity))
            def _(wr_ref=wr_ref, rd_ref=rd_ref, epilogue=epilogue):
                epilogue(lambda k: rd_ref[:, k * HEAD_DIM:(k + 1) * HEAD_DIM])
                wr_ref[...] = jnp.dot(h_ref[...], w_ref[...], preferred_element_type=F32)


def _inproj_call(x2, mod, w, layer, rope_tabs, gains, seq):
    t, d = x2.shape
    tm, tn = INPROJ_TM, INPROJ_TN
    assert 2 * GA * HEAD_DIM == tn and COL_QB * HEAD_DIM % tn == 0
    tiles_per_seq = seq // tm
    n_tiles = IN_COLS // tn
    n_steps = (t // tm) * n_tiles

    def cur(s):
        return jnp.minimum(s, n_steps - 1)

    def prev(s):
        return jnp.maximum(s - 1, 0)

    return pl.pallas_call(
        functools.partial(_inproj_kernel, n_tiles=n_tiles),
        grid=(n_steps + 1,),
        in_specs=[
            pl.BlockSpec((tm, d), lambda s: (cur(s) // n_tiles, 0)),
            pl.BlockSpec((None, mod.shape[1], d),
                         lambda s: (cur(s) // n_tiles // tiles_per_seq, 0, 0)),
            pl.BlockSpec((None, d, tn), lambda s: (layer, 0, cur(s) % n_tiles)),
            pl.BlockSpec((rope_tabs.shape[0], tm, HEAD_DIM),
                         lambda s: (0, (prev(s) // n_tiles) % tiles_per_seq, 0)),
            pl.BlockSpec(gains.shape, lambda s: (0, 0)),
        ],
        out_specs=pl.BlockSpec((tm, tn), lambda s: (prev(s) // n_tiles, prev(s) % n_tiles)),
        out_shape=jax.ShapeDtypeStruct((t, IN_COLS), BF16),
        scratch_shapes=[pltpu.VMEM((tm, d), BF16), pltpu.VMEM((tm, tn), F32),
                        pltpu.VMEM((tm, tn), F32)],
        compiler_params=_cparams(1),
        name="in_proj",
    )(x2, mod, w, rope_tabs, gains)


def _win_kernel(sink_ref, q_ref, k_ref, v_ref, o_ref, mask_ref, *, seq):
    b = pl.program_id(0)
    g = pl.program_id(1)
    i = pl.program_id(2)
    band = 3 * WIN
    rows = RA * WIN
    row = lax.broadcasted_iota(jnp.int32, (rows, 1), 0)

    @pl.when((b == 0) & (g == 0) & (i == 0))
    def _():
        rel0 = lax.broadcasted_iota(jnp.int32, (1, band), 1) - (row & (WIN - 1))
        for kind in range(3):
            mask_ref[kind] = jnp.where(jnp.abs(rel0 - kind * WIN) <= WIN, 0.0, NEG_INF)

    snk = jnp.full((rows, 1), sink_ref[g * RA + RA - 1], F32)
    for r in range(RA - 2, -1, -1):
        snk = jnp.where(row < (r + 1) * WIN, sink_ref[g * RA + r], snk)
    n_blocks = ATT_TQ // WIN
    for j0 in range(0, n_blocks, WIN_GROUP):
        group = range(j0, min(j0 + WIN_GROUP, n_blocks))
        starts, scores, probs, denoms = [], [], [], []
        for j in group:
            blk = i * n_blocks + j
            start = pl.multiple_of(jnp.clip((blk - 1) * WIN, 0, seq - band), WIN)
            starts.append(start)
            qs = jnp.concatenate(
                [q_ref[j * WIN:(j + 1) * WIN, r * HEAD_DIM:(r + 1) * HEAD_DIM] for r in range(RA)],
                axis=0)
            s = lax.dot_general(qs, k_ref[pl.ds(start, band), :], (((1,), (1,)), ((), ())),
                                preferred_element_type=F32)
            scores.append(s + mask_ref[(blk * WIN - start) // WIN])
        for s in scores:
            m = jnp.maximum(jnp.max(s, axis=-1, keepdims=True), snk)
            p = jnp.exp(s - m)
            denoms.append(jnp.sum(p, axis=-1, keepdims=True) + jnp.exp(snk - m))
            probs.append(p.astype(BF16))
        for idx, j in enumerate(group):
            o = jnp.dot(probs[idx], v_ref[pl.ds(starts[idx], band), :],
                        preferred_element_type=F32) / denoms[idx]
            for r in range(RA):
                o_ref[j * WIN:(j + 1) * WIN, r * HEAD_DIM:(r + 1) * HEAD_DIM] = (
                    o[r * WIN:(r + 1) * WIN].astype(BF16))


def _win_call(qkvg, sink, batch, seq):
    t = qkvg.shape[0]
    tq = ATT_TQ
    per_seq = seq // tq
    gw = RA * HEAD_DIM
    return pl.pallas_call(
        functools.partial(_win_kernel, seq=seq),
        grid=(batch, GA, per_seq),
        in_specs=[
            pl.BlockSpec(memory_space=pltpu.SMEM),
            pl.BlockSpec((tq, gw), lambda b, g, i: (b * per_seq + i, g)),
            pl.BlockSpec((seq, HEAD_DIM), lambda b, g, i: (b, COL_KA + g)),
            pl.BlockSpec((seq, HEAD_DIM), lambda b, g, i: (b, COL_VA + g)),
        ],
        out_specs=pl.BlockSpec((tq, gw), lambda b, g, i: (b * per_seq + i, g)),
        out_shape=jax.ShapeDtypeStruct((t, HA * HEAD_DIM), BF16),
        scratch_shapes=[pltpu.VMEM((3, RA * WIN, 3 * WIN), F32)],
        compiler_params=_cparams(3),
        name="win_attn",
    )(sink, qkvg, qkvg, qkvg)


def _nbr_kernel(tab_ref, q_ref, k_ref, v_ref, o_ref, bias_ref, tz_ref, *, grid_rows):
    h = pl.program_id(0)
    b = pl.program_id(1)
    i = pl.program_id(2)
    w = GRID_W
    keys = NA_KH * w

    @pl.when((b == 0) & (i == 0))
    def _():
        qi = lax.broadcasted_iota(jnp.int32, (w, 2 * w), 0)
        lane = lax.broadcasted_iota(jnp.int32, (w, 2 * w), 1)
        kc = lane & (w - 1)
        diff = kc - qi
        c0 = jnp.clip(qi - NA_KW // 2, 0, w - NA_KW)
        valid = (kc >= c0) & (kc < c0 + NA_KW)
        base = h * (N_DR * N_DC)
        for dr in range(N_DR):
            tz = jnp.zeros((w, 2 * w), F32)
            for dc in range(N_DC):
                tz = jnp.where(diff == dc - (NA_KW - 1), tab_ref[base + dr * N_DC + dc], tz)
            tz_ref[dr] = jnp.where(valid, tz, NEG_INF)
        low = lane < w
        for delta in range(NA_KH):
            for c in range(NA_KH // 2):
                bias_ref[delta, :, c * 2 * w:(c + 1) * 2 * w] = jnp.where(
                    low, tz_ref[2 * c - delta + NA_KH - 1], tz_ref[2 * c + 1 - delta + NA_KH - 1])

    n_rows = NBR_TQ // w
    for g0 in range(0, n_rows, NBR_GROUP):
        group = range(g0, min(g0 + NBR_GROUP, n_rows))
        offs, scores, probs, denoms = [], [], [], []
        for rr in group:
            r = i * n_rows + rr
            r0 = jnp.clip(r - NA_KH // 2, 0, grid_rows - NA_KH)
            off = pl.multiple_of(r0 * w, w)
            offs.append(off)
            s = lax.dot_general(q_ref[rr * w:(rr + 1) * w, :], k_ref[pl.ds(off, keys), :],
                                (((1,), (1,)), ((), ())), preferred_element_type=F32)
            scores.append(s + bias_ref[r - r0])
        for s in scores:
            p = jnp.exp(s - jnp.max(s, axis=-1, keepdims=True))
            denoms.append(jnp.sum(p, axis=-1, keepdims=True))
            probs.append(p.astype(BF16))
        for idx, rr in enumerate(group):
            o = jnp.dot(probs[idx], v_ref[pl.ds(offs[idx], keys), :], preferred_element_type=F32)
            o_ref[rr * w:(rr + 1) * w, :] = (o / denoms[idx]).astype(BF16)


def _nbr_call(qkvg, tab_flat, batch, seq):
    t = qkvg.shape[0]
    tq = NBR_TQ
    per_seq = seq // tq
    grid_rows = seq // GRID_W
    assert grid_rows >= NA_KH
    return pl.pallas_call(
        functools.partial(_nbr_kernel, grid_rows=grid_rows),
        grid=(HB, batch, per_seq),
        in_specs=[
            pl.BlockSpec(memory_space=pltpu.SMEM),
            pl.BlockSpec((tq, HEAD_DIM), lambda h, b, i: (b * per_seq + i, COL_QB + h)),
            pl.BlockSpec((seq, HEAD_DIM), lambda h, b, i: (b, COL_KB + h)),
            pl.BlockSpec((seq, HEAD_DIM), lambda h, b, i: (b, COL_VB + h)),
        ],
        out_specs=pl.BlockSpec((tq, HEAD_DIM), lambda h, b, i: (b * per_seq + i, h)),
        out_shape=jax.ShapeDtypeStruct((t, HB * HEAD_DIM), BF16),
        scratch_shapes=[pltpu.VMEM((NA_KH, GRID_W, NA_KH * GRID_W), F32),
                        pltpu.VMEM((N_DR, GRID_W, 2 * GRID_W), F32)],
        compiler_params=_cparams(3),
        name="nbr_attn",
    )(tab_flat, qkvg, qkvg, qkvg)


def _outproj_kernel(oa_ref, ob_ref, ga_ref, gb_ref, wpa_ref, wpb_ref, wo_ref, x_ref, gt_ref,
                    o_ref):
    c = pl.program_id(1)

    @pl.when(c == 0)
    def _():
        o_ref[...] = jnp.zeros_like(o_ref)

    ya = jnp.dot(oa_ref[...], wpa_ref[...], preferred_element_type=F32)
    yb = jnp.dot(ob_ref[...], wpb_ref[...], preferred_element_type=F32)
    merged = (ga_ref[...].astype(F32) * ya + gb_ref[...].astype(F32) * yb).astype(BF16)
    o_ref[...] += jnp.dot(merged, wo_ref[...], preferred_element_type=F32)

    @pl.when(c == pl.num_programs(1) - 1)
    def _():
        o_ref[...] = x_ref[...] + gt_ref[...] * o_ref[...]


def _outproj_call(oa, ob, qkvg, wpa, wpb, wo, layer, x2, gt, seq):
    t, d = x2.shape
    tm, tc = OUT_TM, OUT_TC
    ga0 = COL_GA * HEAD_DIM // tc
    gb0 = COL_GB * HEAD_DIM // tc
    assert ga0 * tc == COL_GA * HEAD_DIM and gb0 * tc == COL_GB * HEAD_DIM
    tiles_per_seq = seq // tm
    return pl.pallas_call(
        _outproj_kernel,
        grid=(t // tm, d // tc),
        in_specs=[
            pl.BlockSpec((tm, oa.shape[1]), lambda i, c: (i, 0)),
            pl.BlockSpec((tm, ob.shape[1]), lambda i, c: (i, 0)),
            pl.BlockSpec((tm, tc), lambda i, c: (i, ga0 + c)),
            pl.BlockSpec((tm, tc), lambda i, c: (i, gb0 + c)),
            pl.BlockSpec((None, wpa.shape[1], tc), lambda i, c: (layer, 0, c)),
            pl.BlockSpec((None, wpb.shape[1], tc), lambda i, c: (layer, 0, c)),
            pl.BlockSpec((None, tc, d), lambda i, c: (layer, c, 0)),
            pl.BlockSpec((tm, d), lambda i, c: (i, 0)),
            pl.BlockSpec((None, 1, d), lambda i, c: (i // tiles_per_seq, 0, 0)),
        ],
        out_specs=pl.BlockSpec((tm, d), lambda i, c: (i, 0)),
        out_shape=jax.ShapeDtypeStruct((t, d), F32),
        compiler_params=_cparams(2),
        name="out_proj",
    )(oa, ob, qkvg, qkvg, wpa, wpb, wo, x2, gt)


def _ffn_kernel(x_ref, xp_ref, xn_ref, mod_ref, wg_ref, wv_ref, cg_ref, cv_ref, wd_ref,
                o_ref, h_ref, acc_ref, u_ref, *, tiles_per_seq):
    i = pl.program_id(0)
    j = pl.program_id(1)
    tm = FFN_TM
    ext = tm + 2 * HALO
    n_taps = cg_ref.shape[0] - 1

    def normalise():
        def modulated(x):
            return _rms(x) * mod_ref[0:1, :] + mod_ref[1:2, :]

        pos = i % tiles_per_seq
        keep_prev = jnp.where(pos == 0, 0.0, 1.0)
        keep_next = jnp.where(pos == tiles_per_seq - 1, 0.0, 1.0)
        h_ref[0:HALO, :] = (modulated(xp_ref[...]) * keep_prev).astype(BF16)
        h_ref[HALO:HALO + tm, :] = modulated(x_ref[...]).astype(BF16)
        h_ref[HALO + tm:ext, :] = (modulated(xn_ref[...]) * keep_next).astype(BF16)

    def conv(slot, c_ref):
        return (u_ref[slot, HALO - 1:HALO - 1 + tm, :] * c_ref[0:1, :]
                + u_ref[slot, HALO:HALO + tm, :] * c_ref[1:2, :]
                + u_ref[slot, HALO + 1:HALO + 1 + tm, :] * c_ref[2:3, :]
                + c_ref[n_taps:n_taps + 1, :])

    def part():
        h = h_ref[...]
        u_ref[0] = jnp.dot(h, wg_ref[...], preferred_element_type=F32)
        u_ref[1] = jnp.dot(h, wv_ref[...], preferred_element_type=F32)
        gate = conv(0, cg_ref)
        val = conv(1, cv_ref)
        act = (gate * _sigmoid(gate) * val).astype(BF16)
        return jnp.dot(act, wd_ref[...], preferred_element_type=F32)

    last = pl.num_programs(1) - 1

    @pl.when(j == 0)
    def _():
        normalise()
        acc_ref[...] = part()

    @pl.when((j > 0) & (j < last))
    def _():
        acc_ref[...] += part()

    @pl.when(j == last)
    def _():
        o_ref[...] = x_ref[...] + mod_ref[2:3, :] * (acc_ref[...] + part())


def _ffn_call(x2, mod, w_up, conv_p, w_down, layer, seq):
    t, d = x2.shape
    tm, tf = FFN_TM, FFN_TF
    d_ff = w_down.shape[1]
    n_rows = conv_p.shape[1]
    assert n_rows == 4
    nf = d_ff // tf
    tiles_per_seq = seq // tm
    halo_per_tile = tm // HALO
    n_halo = t // HALO
    return pl.pallas_call(
        functools.partial(_ffn_kernel, tiles_per_seq=tiles_per_seq),
        grid=(t // tm, nf),
        in_specs=[
            pl.BlockSpec((tm, d), lambda i, j: (i, 0)),
            pl.BlockSpec((HALO, d), lambda i, j: (jnp.maximum(i * halo_per_tile - 1, 0), 0)),
            pl.BlockSpec((HALO, d),
                         lambda i, j: (jnp.minimum((i + 1) * halo_per_tile, n_halo - 1), 0)),
            pl.BlockSpec((None, mod.shape[1], d), lambda i, j: (i // tiles_per_seq, 0, 0)),
            pl.BlockSpec((None, d, tf), lambda i, j: (layer, 0, j)),
            pl.BlockSpec((None, d, tf), lambda i, j: (layer, 0, j + nf)),
            pl.BlockSpec((None, n_rows, tf), lambda i, j: (layer, 0, j)),
            pl.BlockSpec((None, n_rows, tf), lambda i, j: (layer, 0, j + nf)),
            pl.BlockSpec((None, tf, d), lambda i, j: (layer, j, 0)),
        ],
        out_specs=pl.BlockSpec((tm, d), lambda i, j: (i, 0)),
        out_shape=jax.ShapeDtypeStruct((t, d), F32),
        scratch_shapes=[pltpu.VMEM((tm + 2 * HALO, d), BF16), pltpu.VMEM((tm, d), F32),
                        pltpu.VMEM((2, tm + 2 * HALO, tf), F32)],
        compiler_params=_cparams(2),
        name="conv_ffn",
    )(x2, x2, x2, mod, w_up, w_up, conv_p, conv_p, w_down)


def _rope_tables(seq):
    half = ROT_DIM // 2
    pos = jnp.arange(seq, dtype=F32)
    inv = jnp.float32(ROPE_THETA) ** (-jnp.arange(0, ROT_DIM, 2, dtype=F32) / ROT_DIM)
    ang = pos[:, None] * inv[None, :]
    cos, sin = jnp.cos(ang), jnp.sin(ang)
    rest = HEAD_DIM - ROT_DIM
    cos_t = jnp.concatenate([cos, cos, jnp.ones((seq, rest), F32)], axis=1)
    sin_up = jnp.concatenate([jnp.zeros((seq, half), F32), sin, jnp.zeros((seq, rest), F32)], axis=1)
    sin_dn = jnp.concatenate([-sin, jnp.zeros((seq, HEAD_DIM - half), F32)], axis=1)
    return jnp.stack([cos_t, sin_up, sin_dn])


def kernel(x, c, ada_w, ada_b, norm_mix, norm_ffn, w_in, qn_a, kn_a, qn_b, kn_b, sink_a,
           rel_bias_b, w_proj_a, w_proj_b, w_out, w_up, conv_w, conv_b, w_down):
    batch, seq, d = x.shape
    n_layers = ada_w.shape[0]
    t = batch * seq
    x2 = x.reshape(t, d)

    c_pad = jnp.pad(c, ((0, 8 - batch % 8 if batch % 8 else 0), (0, 0)))
    mod = _ada_call(c_pad, ada_w, ada_b)[:, :batch]
    mod = mod.reshape(n_layers, batch, 6, d)
    rope_tabs = _rope_tables(seq)

    w_in_b, w_pa_b, w_pb_b, w_out_b, w_up_b, w_down_b = [
        w.astype(BF16) for w in (w_in, w_proj_a, w_proj_b, w_out, w_up, w_down)]
    conv_p = jnp.concatenate([conv_w, conv_b[:, None, :]], axis=1)

    for l in range(n_layers):
        sh_a, sc_a, gt_a, sh_m, sc_m, gt_m = [mod[l, :, k] for k in range(6)]
        mod_mix = jnp.stack([norm_mix[l] * (1.0 + sc_a), sh_a], axis=1)
        mod_ffn = jnp.stack([norm_ffn[l] * (1.0 + sc_m), sh_m, gt_m], axis=1)
        gains = jnp.stack([qn_a[l] * QK_SCALE, kn_a[l], qn_b[l] * QK_SCALE, kn_b[l]])
        qkvg = _inproj_call(x2, mod_mix, w_in_b, l, rope_tabs, gains, seq)
        oa = _win_call(qkvg, sink_a[l], batch, seq)
        ob = _nbr_call(qkvg, rel_bias_b[l].reshape(-1), batch, seq)
        x2 = _outproj_call(oa, ob, qkvg, w_pa_b, w_pb_b, w_out_b, l, x2, gt_a[:, None, :], seq)
        x2 = _ffn_call(x2, mod_ffn, w_up_b, conv_p, w_down_b, l, seq)
    return x2.reshape(batch, seq, d)
```

```python
import functools

import jax
import jax.numpy as jnp
from jax import lax
from jax.experimental import pallas as pl
from jax.experimental.pallas import tpu as pltpu

F32 = jnp.float32
BF16 = jnp.bfloat16

D_MODEL = 2048
HEAD_DIM = 128
HA = 8
GA = 2
RA = HA // GA
HB = 8
WIN = 128
GRID_W = 64
NA_KH = 8
NA_KW = 16
ROT_DIM = HEAD_DIM // 4
ROPE_THETA = 500000.0
D_FF = 5632
EPS = 1e-6
NEG_INF = -1e30
QK_SCALE = HEAD_DIM ** -0.5
IN_COLS = HA * HEAD_DIM + 2 * GA * HEAD_DIM + 3 * HB * HEAD_DIM + 2 * D_MODEL

COL_QA = 0
COL_KA = COL_QA + HA
COL_VA = COL_KA + GA
COL_QB = COL_VA + GA
COL_KB = COL_QB + HB
COL_VB = COL_KB + HB
COL_GA = COL_VB + HB
COL_GB = COL_GA + D_MODEL // HEAD_DIM

VMEM_LIMIT = 56 * 1024 * 1024
ADA_TN = 1024
INPROJ_TM = 1024
INPROJ_TN = 512
ATT_TQ = 2048
WIN_GROUP = 8
NBR_TQ = 4096
NBR_GROUP = 64
OUT_TM = 1024
OUT_TC = 512
FFN_TM = 512
FFN_TF = 512
HALO = 16
N_DR = 2 * NA_KH - 1
N_DC = 2 * NA_KW - 1


def _cparams(n_axes):
    return pltpu.CompilerParams(dimension_semantics=("arbitrary",) * n_axes,
                                vmem_limit_bytes=VMEM_LIMIT)


def _sigmoid(x):
    return 1.0 / (1.0 + jnp.exp(-x))


def _rms(x):
    return x * lax.rsqrt(jnp.mean(x * x, axis=-1, keepdims=True) + EPS)


def _ada_kernel(c_ref, w_ref, b_ref, o_ref):
    c = c_ref[...]
    c_act = (c * _sigmoid(c)).astype(BF16)
    o_ref[...] = jnp.dot(c_act, w_ref[...].astype(BF16), preferred_element_type=F32) + b_ref[...]


def _ada_call(c_pad, ada_w, ada_b):
    n_layers, d, six_d = ada_w.shape
    rows = c_pad.shape[0]
    return pl.pallas_call(
        _ada_kernel,
        grid=(n_layers, six_d // ADA_TN),
        in_specs=[
            pl.BlockSpec((rows, d), lambda l, n: (0, 0)),
            pl.BlockSpec((None, d, ADA_TN), lambda l, n: (l, 0, n)),
            pl.BlockSpec((None, 1, ADA_TN), lambda l, n: (l, 0, n)),
        ],
        out_specs=pl.BlockSpec((None, rows, ADA_TN), lambda l, n: (l, 0, n)),
        out_shape=jax.ShapeDtypeStruct((n_layers, rows, six_d), F32),
        compiler_params=_cparams(2),
        name="ada_mod",
    )(c_pad, ada_w, ada_b.reshape(n_layers, 1, six_d))


def _rope(a, cos, sin_up, sin_dn):
    half = ROT_DIM // 2
    return (a * cos + pltpu.roll(a, half, 1) * sin_up
            + pltpu.roll(a, HEAD_DIM - half, 1) * sin_dn)


def _inproj_kernel(x_ref, mod_ref, w_ref, rope_ref, gains_ref, o_ref, h_ref, acc0_ref, acc1_ref, *,
                   n_tiles):
    s = pl.program_id(0)
    n_steps = pl.num_programs(0) - 1
    heads_per_tile = INPROJ_TN // HEAD_DIM
    n_cur = jnp.minimum(s, n_steps - 1) % n_tiles
    n = jnp.maximum(s - 1, 0) % n_tiles
    slot = s % 2

    @pl.when((n_cur == 0) & (s < n_steps))
    def _():
        h_ref[...] = (_rms(x_ref[...]) * mod_ref[0:1, :] + mod_ref[1:2, :]).astype(BF16)

    @pl.when(s == 0)
    def _():
        acc1_ref[...] = jnp.zeros_like(acc1_ref)

    def put(k, val):
        o_ref[:, k * HEAD_DIM:(k + 1) * HEAD_DIM] = val.astype(BF16)

    def rope(a):
        return _rope(a, rope_ref[0], rope_ref[1], rope_ref[2])

    qna_ref, kna_ref, qnb_ref, knb_ref = [gains_ref.at[k:k + 1, :] for k in range(4)]

    n_qa = HA // heads_per_tile
    t_kva = n_qa
    t_qb = COL_QB // heads_per_tile
    t_kb = COL_KB // heads_per_tile
    t_vb = COL_VB // heads_per_tile
    t_gate = COL_GA // heads_per_tile

    def epi_qa(head):
        for k in range(heads_per_tile):
            put(k, rope(_rms(head(k)) * qna_ref[...]))

    def epi_kva(head):
        for k in range(GA):
            put(k, rope(_rms(head(k)) * kna_ref[...]))
        for k in range(GA, heads_per_tile):
            put(k, head(k))

    def epi_qb(head):
        for k in range(heads_per_tile):
            put(k, _rms(head(k)) * qnb_ref[...])

    def epi_kb(head):
        for k in range(heads_per_tile):
            put(k, _rms(head(k)) * knb_ref[...])

    def epi_vb(head):
        for k in range(heads_per_tile):
            put(k, head(k))

    def epi_gate(head):
        for k in range(heads_per_tile):
            put(k, _sigmoid(head(k)))

    kinds = [
        (n < n_qa, epi_qa),
        (n == t_kva, epi_kva),
        ((n >= t_qb) & (n < t_kb), epi_qb),
        ((n >= t_kb) & (n < t_vb), epi_kb),
        ((n >= t_vb) & (n < t_gate), epi_vb),
        (n >= t_gate, epi_gate),
    ]

    for parity, (wr_ref, rd_ref) in enumerate(((acc0_ref, acc1_ref), (acc1_ref, acc0_ref))):
        for cond, epilogue in kinds:
            @pl.when(cond & (slot == parity))
            def _(wr_ref=wr_ref, rd_ref=rd_ref, epilogue=epilogue):
                epilogue(lambda k: rd_ref[:, k * HEAD_DIM:(k + 1) * HEAD_DIM])
                wr_ref[...] = jnp.dot(h_ref[...], w_ref[...], preferred_element_type=F32)


def _inproj_call(x2, mod, w, layer, rope_tabs, gains, seq):
    t, d = x2.shape
    tm, tn = INPROJ_TM, INPROJ_TN
    assert 2 * GA * HEAD_DIM == tn and COL_QB * HEAD_DIM % tn == 0
    tiles_per_seq = seq // tm
    n_tiles = IN_COLS // tn
    n_steps = (t // tm) * n_tiles

    def cur(s):
        return jnp.minimum(s, n_steps - 1)

    def prev(s):
        return jnp.maximum(s - 1, 0)

    return pl.pallas_call(
        functools.partial(_inproj_kernel, n_tiles=n_tiles),
        grid=(n_steps + 1,),
        in_specs=[
            pl.BlockSpec((tm, d), lambda s: (cur(s) // n_tiles, 0)),
            pl.BlockSpec((None, mod.shape[1], d),
                         lambda s: (cur(s) // n_tiles // tiles_per_seq, 0, 0)),
            pl.BlockSpec((None, d, tn), lambda s: (layer, 0, cur(s) % n_tiles)),
            pl.BlockSpec((rope_tabs.shape[0], tm, HEAD_DIM),
                         lambda s: (0, (prev(s) // n_tiles) % tiles_per_seq, 0)),
            pl.BlockSpec(gains.shape, lambda s: (0, 0)),
        ],
        out_specs=pl.BlockSpec((tm, tn), lambda s: (prev(s) // n_tiles, prev(s) % n_tiles)),
        out_shape=jax.ShapeDtypeStruct((t, IN_COLS), BF16),
        scratch_shapes=[pltpu.VMEM((tm, d), BF16), pltpu.VMEM((tm, tn), F32),
                        pltpu.VMEM((tm, tn), F32)],
        compiler_params=_cparams(1),
        name="in_proj",
    )(x2, mod, w, rope_tabs, gains)


def _win_kernel(sink_ref, q_ref, k_ref, v_ref, o_ref, mask_ref, *, seq):
    b = pl.program_id(0)
    g = pl.program_id(1)
    i = pl.program_id(2)
    band = 3 * WIN
    rows = RA * WIN
    row = lax.broadcasted_iota(jnp.int32, (rows, 1), 0)

    @pl.when((b == 0) & (g == 0) & (i == 0))
    def _():
        rel0 = lax.broadcasted_iota(jnp.int32, (1, band), 1) - (row & (WIN - 1))
        for kind in range(3):
            mask_ref[kind] = jnp.where(jnp.abs(rel0 - kind * WIN) <= WIN, 0.0, NEG_INF)

    snk = jnp.full((rows, 1), sink_ref[g * RA + RA - 1], F32)
    for r in range(RA - 2, -1, -1):
        snk = jnp.where(row < (r + 1) * WIN, sink_ref[g * RA + r], snk)
    n_blocks = ATT_TQ // WIN
    for j0 in range(0, n_blocks, WIN_GROUP):
        group = range(j0, min(j0 + WIN_GROUP, n_blocks))
        starts, scores, probs, denoms = [], [], [], []
        for j in group:
            blk = i * n_blocks + j
            start = pl.multiple_of(jnp.clip((blk - 1) * WIN, 0, seq - band), WIN)
            starts.append(start)
            qs = jnp.concatenate(
                [q_ref[j * WIN:(j + 1) * WIN, r * HEAD_DIM:(r + 1) * HEAD_DIM] for r in range(RA)],
                axis=0)
            s = lax.dot_general(qs, k_ref[pl.ds(start, band), :], (((1,), (1,)), ((), ())),
                                preferred_element_type=F32)
            scores.append(s + mask_ref[(blk * WIN - start) // WIN])
        for s in scores:
            m = jnp.maximum(jnp.max(s, axis=-1, keepdims=True), snk)
            p = jnp.exp(s - m)
            denoms.append(jnp.sum(p, axis=-1, keepdims=True) + jnp.exp(snk - m))
            probs.append(p.astype(BF16))
        for idx, j in enumerate(group):
            o = jnp.dot(probs[idx], v_ref[pl.ds(starts[idx], band), :],
                        preferred_element_type=F32) / denoms[idx]
            for r in range(RA):
                o_ref[j * WIN:(j + 1) * WIN, r * HEAD_DIM:(r + 1) * HEAD_DIM] = (
                    o[r * WIN:(r + 1) * WIN].astype(BF16))


def _win_call(qkvg, sink, batch, seq):
    t = qkvg.shape[0]
    tq = ATT_TQ
    per_seq = seq // tq
    gw = RA * HEAD_DIM
    return pl.pallas_call(
        functools.partial(_win_kernel, seq=seq),
        grid=(batch, GA, per_seq),
        in_specs=[
            pl.BlockSpec(memory_space=pltpu.SMEM),
            pl.BlockSpec((tq, gw), lambda b, g, i: (b * per_seq + i, g)),
            pl.BlockSpec((seq, HEAD_DIM), lambda b, g, i: (b, COL_KA + g)),
            pl.BlockSpec((seq, HEAD_DIM), lambda b, g, i: (b, COL_VA + g)),
        ],
        out_specs=pl.BlockSpec((tq, gw), lambda b, g, i: (b * per_seq + i, g)),
        out_shape=jax.ShapeDtypeStruct((t, HA * HEAD_DIM), BF16),
        scratch_shapes=[pltpu.VMEM((3, RA * WIN, 3 * WIN), F32)],
        compiler_params=_cparams(3),
        name="win_attn",
    )(sink, qkvg, qkvg, qkvg)


def _nbr_kernel(tab_ref, q_ref, k_ref, v_ref, o_ref, bias_ref, tz_ref, *, grid_rows):
    h = pl.program_id(0)
    b = pl.program_id(1)
    i = pl.program_id(2)
    w = GRID_W
    keys = NA_KH * w

    @pl.when((b == 0) & (i == 0))
    def _():
        qi = lax.broadcasted_iota(jnp.int32, (w, 2 * w), 0)
        lane = lax.broadcasted_iota(jnp.int32, (w, 2 * w), 1)
        kc = lane & (w - 1)
        diff = kc - qi
        c0 = jnp.clip(qi - NA_KW // 2, 0, w - NA_KW)
        valid = (kc >= c0) & (kc < c0 + NA_KW)
        base = h * (N_DR * N_DC)
        for dr in range(N_DR):
            tz = jnp.zeros((w, 2 * w), F32)
            for dc in range(N_DC):
                tz = jnp.where(diff == dc - (NA_KW - 1), tab_ref[base + dr * N_DC + dc], tz)
            tz_ref[dr] = jnp.where(valid, tz, NEG_INF)
        low = lane < w
        for delta in range(NA_KH):
            for c in range(NA_KH // 2):
                bias_ref[delta, :, c * 2 * w:(c + 1) * 2 * w] = jnp.where(
                    low, tz_ref[2 * c - delta + NA_KH - 1], tz_ref[2 * c + 1 - delta + NA_KH - 1])

    n_rows = NBR_TQ // w
    for g0 in range(0, n_rows, NBR_GROUP):
        group = range(g0, min(g0 + NBR_GROUP, n_rows))
        offs, scores, probs, denoms = [], [], [], []
        for rr in group:
            r = i * n_rows + rr
            r0 = jnp.clip(r - NA_KH // 2, 0, grid_rows - NA_KH)
            off = pl.multiple_of(r0 * w, w)
            offs.append(off)
            s = lax.dot_general(q_ref[rr * w:(rr + 1) * w, :], k_ref[pl.ds(off, keys), :],
                                (((1,), (1,)), ((), ())), preferred_element_type=F32)
            scores.append(s + bias_ref[r - r0])
        for s in scores:
            p = jnp.exp(s - jnp.max(s, axis=-1, keepdims=True))
            denoms.append(jnp.sum(p, axis=-1, keepdims=True))
            probs.append(p.astype(BF16))
        for idx, rr in enumerate(group):
            o = jnp.dot(probs[idx], v_ref[pl.ds(offs[idx], keys), :], preferred_element_type=F32)
            o_ref[rr * w:(rr + 1) * w, :] = (o / denoms[idx]).astype(BF16)


def _nbr_call(qkvg, tab_flat, batch, seq):
    t = qkvg.shape[0]
    tq = NBR_TQ
    per_seq = seq // tq
    grid_rows = seq // GRID_W
    assert grid_rows >= NA_KH
    return pl.pallas_call(
        functools.partial(_nbr_kernel, grid_rows=grid_rows),
        grid=(HB, batch, per_seq),
        in_specs=[
            pl.BlockSpec(memory_space=pltpu.SMEM),
            pl.BlockSpec((tq, HEAD_DIM), lambda h, b, i: (b * per_seq + i, COL_QB + h)),
            pl.BlockSpec((seq, HEAD_DIM), lambda h, b, i: (b, COL_KB + h)),
            pl.BlockSpec((seq, HEAD_DIM), lambda h, b, i: (b, COL_VB + h)),
        ],
        out_specs=pl.BlockSpec((tq, HEAD_DIM), lambda h, b, i: (b * per_seq + i, h)),
        out_shape=jax.ShapeDtypeStruct((t, HB * HEAD_DIM), BF16),
        scratch_shapes=[pltpu.VMEM((NA_KH, GRID_W, NA_KH * GRID_W), F32),
                        pltpu.VMEM((N_DR, GRID_W, 2 * GRID_W), F32)],
        compiler_params=_cparams(3),
        name="nbr_attn",
    )(tab_flat, qkvg, qkvg, qkvg)


def _outproj_kernel(oa_ref, ob_ref, ga_ref, gb_ref, wpa_ref, wpb_ref, wo_ref, x_ref, gt_ref,
                    o_ref):
    c = pl.program_id(1)

    @pl.when(c == 0)
    def _():
        o_ref[...] = jnp.zeros_like(o_ref)

    ya = jnp.dot(oa_ref[...], wpa_ref[...], preferred_element_type=F32)
    yb = jnp.dot(ob_ref[...], wpb_ref[...], preferred_element_type=F32)
    merged = (ga_ref[...].astype(F32) * ya + gb_ref[...].astype(F32) * yb).astype(BF16)
    o_ref[...] += jnp.dot(merged, wo_ref[...], preferred_element_type=F32)

    @pl.when(c == pl.num_programs(1) - 1)
    def _():
        o_ref[...] = x_ref[...] + gt_ref[...] * o_ref[...]


def _outproj_call(oa, ob, qkvg, wpa, wpb, wo, layer, x2, gt, seq):
    t, d = x2.shape
    tm, tc = OUT_TM, OUT_TC
    ga0 = COL_GA * HEAD_DIM // tc
    gb0 = COL_GB * HEAD_DIM // tc
    assert ga0 * tc == COL_GA * HEAD_DIM and gb0 * tc == COL_GB * HEAD_DIM
    tiles_per_seq = seq // tm
    return pl.pallas_call(
        _outproj_kernel,
        grid=(t // tm, d // tc),
        in_specs=[
            pl.BlockSpec((tm, oa.shape[1]), lambda i, c: (i, 0)),
            pl.BlockSpec((tm, ob.shape[1]), lambda i, c: (i, 0)),
            pl.BlockSpec((tm, tc), lambda i, c: (i, ga0 + c)),
            pl.BlockSpec((tm, tc), lambda i, c: (i, gb0 + c)),
            pl.BlockSpec((None, wpa.shape[1], tc), lambda i, c: (layer, 0, c)),
            pl.BlockSpec((None, wpb.shape[1], tc), lambda i, c: (layer, 0, c)),
            pl.BlockSpec((None, tc, d), lambda i, c: (layer, c, 0)),
            pl.BlockSpec((tm, d), lambda i, c: (i, 0)),
            pl.BlockSpec((None, 1, d), lambda i, c: (i // tiles_per_seq, 0, 0)),
        ],
        out_specs=pl.BlockSpec((tm, d), lambda i, c: (i, 0)),
        out_shape=jax.ShapeDtypeStruct((t, d), F32),
        compiler_params=_cparams(2),
        name="out_proj",
    )(oa, ob, qkvg, qkvg, wpa, wpb, wo, x2, gt)


def _ffn_kernel(x_ref, xp_ref, xn_ref, mod_ref, wg_ref, wv_ref, cg_ref, cv_ref, wd_ref,
                o_ref, h_ref, acc_ref, u_ref, *, tiles_per_seq):
    i = pl.program_id(0)
    j = pl.program_id(1)
    tm = FFN_TM
    ext = tm + 2 * HALO
    n_taps = cg_ref.shape[0] - 1

    def normalise():
        def modulated(x):
            return _rms(x) * mod_ref[0:1, :] + mod_ref[1:2, :]

        pos = i % tiles_per_seq
        keep_prev = jnp.where(pos == 0, 0.0, 1.0)
        keep_next = jnp.where(pos == tiles_per_seq - 1, 0.0, 1.0)
        h_ref[0:HALO, :] = (modulated(xp_ref[...]) * keep_prev).astype(BF16)
        h_ref[HALO:HALO + tm, :] = modulated(x_ref[...]).astype(BF16)
        h_ref[HALO + tm:ext, :] = (modulated(xn_ref[...]) * keep_next).astype(BF16)

    def conv(slot, c_ref):
        return (u_ref[slot, HALO - 1:HALO - 1 + tm, :] * c_ref[0:1, :]
                + u_ref[slot, HALO:HALO + tm, :] * c_ref[1:2, :]
                + u_ref[slot, HALO + 1:HALO + 1 + tm, :] * c_ref[2:3, :]
                + c_ref[n_taps:n_taps + 1, :])

    def part():
        h = h_ref[...]
        u_ref[0] = jnp.dot(h, wg_ref[...], preferred_element_type=F32)
        u_ref[1] = jnp.dot(h, wv_ref[...], preferred_element_type=F32)
        gate = conv(0, cg_ref)
        val = conv(1, cv_ref)
        act = (gate * _sigmoid(gate) * val).astype(BF16)
        return jnp.dot(act, wd_ref[...], preferred_element_type=F32)

    last = pl.num_programs(1) - 1

    @pl.when(j == 0)
    def _():
        normalise()
        acc_ref[...] = part()

    @pl.when((j > 0) & (j < last))
    def _():
        acc_ref[...] += part()

    @pl.when(j == last)
    def _():
        o_ref[...] = x_ref[...] + mod_ref[2:3, :] * (acc_ref[...] + part())


def _ffn_call(x2, mod, w_up, conv_p, w_down, layer, seq):
    t, d = x2.shape
    tm, tf = FFN_TM, FFN_TF
    d_ff = w_down.shape[1]
    n_rows = conv_p.shape[1]
    assert n_rows == 4
    nf = d_ff // tf
    assert nf >= 2
    tiles_per_seq = seq // tm
    halo_per_tile = tm // HALO
    n_halo = t // HALO
    return pl.pallas_call(
        functools.partial(_ffn_kernel, tiles_per_seq=tiles_per_seq),
        grid=(t // tm, nf),
        in_specs=[
            pl.BlockSpec((tm, d), lambda i, j: (i, 0)),
            pl.BlockSpec((HALO, d), lambda i, j: (jnp.maximum(i * halo_per_tile - 1, 0), 0)),
            pl.BlockSpec((HALO, d),
                         lambda i, j: (jnp.minimum((i + 1) * halo_per_tile, n_halo - 1), 0)),
            pl.BlockSpec((None, mod.shape[1], d), lambda i, j: (i // tiles_per_seq, 0, 0)),
            pl.BlockSpec((None, d, tf), lambda i, j: (layer, 0, j)),
            pl.BlockSpec((None, d, tf), lambda i, j: (layer, 0, j + nf)),
            pl.BlockSpec((None, n_rows, tf), lambda i, j: (layer, 0, j)),
            pl.BlockSpec((None, n_rows, tf), lambda i, j: (layer, 0, j + nf)),
            pl.BlockSpec((None, tf, d), lambda i, j: (layer, j, 0)),
        ],
        out_specs=pl.BlockSpec((tm, d), lambda i, j: (i, 0)),
        out_shape=jax.ShapeDtypeStruct((t, d), F32),
        scratch_shapes=[pltpu.VMEM((tm + 2 * HALO, d), BF16), pltpu.VMEM((tm, d), F32),
                        pltpu.VMEM((2, tm + 2 * HALO, tf), F32)],
        compiler_params=_cparams(2),
        name="conv_ffn",
    )(x2, x2, x2, mod, w_up, w_up, conv_p, conv_p, w_down)


def _rope_tables(seq):
    half = ROT_DIM // 2
    pos = jnp.arange(seq, dtype=F32)
    inv = jnp.float32(ROPE_THETA) ** (-jnp.arange(0, ROT_DIM, 2, dtype=F32) / ROT_DIM)
    ang = pos[:, None] * inv[None, :]
    cos, sin = jnp.cos(ang), jnp.sin(ang)
    rest = HEAD_DIM - ROT_DIM
    cos_t = jnp.concatenate([cos, cos, jnp.ones((seq, rest), F32)], axis=1)
    sin_up = jnp.concatenate([jnp.zeros((seq, half), F32), sin, jnp.zeros((seq, rest), F32)], axis=1)
    sin_dn = jnp.concatenate([-sin, jnp.zeros((seq, HEAD_DIM - half), F32)], axis=1)
    return jnp.stack([cos_t, sin_up, sin_dn])


def kernel(x, c, ada_w, ada_b, norm_mix, norm_ffn, w_in, qn_a, kn_a, qn_b, kn_b, sink_a,
           rel_bias_b, w_proj_a, w_proj_b, w_out, w_up, conv_w, conv_b, w_down):
    batch, seq, d = x.shape
    n_layers = ada_w.shape[0]
    t = batch * seq
    x2 = x.reshape(t, d)

    c_pad = jnp.pad(c, ((0, 8 - batch % 8 if batch % 8 else 0), (0, 0)))
    mod = _ada_call(c_pad, ada_w, ada_b)[:, :batch]
    mod = mod.reshape(n_layers, batch, 6, d)
    rope_tabs = _rope_tables(seq)

    w_in_b, w_pa_b, w_pb_b, w_out_b, w_up_b, w_down_b = [
        w.astype(BF16) for w in (w_in, w_proj_a, w_proj_b, w_out, w_up, w_down)]
    conv_p = jnp.concatenate([conv_w, conv_b[:, None, :]], axis=1)

    for l in range(n_layers):
        sh_a, sc_a, gt_a, sh_m, sc_m, gt_m = [mod[l, :, k] for k in range(6)]
        mod_mix = jnp.stack([norm_mix[l] * (1.0 + sc_a), sh_a], axis=1)
        mod_ffn = jnp.stack([norm_ffn[l] * (1.0 + sc_m), sh_m, gt_m], axis=1)
        gains = jnp.stack([qn_a[l] * QK_SCALE, kn_a[l], qn_b[l] * QK_SCALE, kn_b[l]])
        qkvg = _inproj_call(x2, mod_mix, w_in_b, l, rope_tabs, gains, seq)
        oa = _win_call(qkvg, sink_a[l], batch, seq)
        ob = _nbr_call(qkvg, rel_bias_b[l].reshape(-1), batch, seq)
        x2 = _outproj_call(oa, ob, qkvg, w_pa_b, w_pb_b, w_out_b, l, x2, gt_a[:, None, :], seq)
        x2 = _ffn_call(x2, mod_ffn, w_up_b, conv_p, w_down_b, l, seq)
    return x2.reshape(batch, seq, d)
```

```python
import functools

import jax
import jax.numpy as jnp
from jax import lax
from jax.experimental import pallas as pl
from jax.experimental.pallas import tpu as pltpu

F32 = jnp.float32
BF16 = jnp.bfloat16

D_MODEL = 2048
HEAD_DIM = 128
HA = 8
GA = 2
RA = HA // GA
HB = 8
WIN = 128
GRID_W = 64
NA_KH = 8
NA_KW = 16
ROT_DIM = HEAD_DIM // 4
ROPE_THETA = 500000.0
D_FF = 5632
EPS = 1e-6
NEG_INF = -1e30
QK_SCALE = HEAD_DIM ** -0.5
IN_COLS = HA * HEAD_DIM + 2 * GA * HEAD_DIM + 3 * HB * HEAD_DIM + 2 * D_MODEL

COL_QA = 0
COL_KA = COL_QA + HA
COL_VA = COL_KA + GA
COL_QB = COL_VA + GA
COL_KB = COL_QB + HB
COL_VB = COL_KB + HB
COL_GA = COL_VB + HB
COL_GB = COL_GA + D_MODEL // HEAD_DIM

VMEM_LIMIT = 56 * 1024 * 1024
ADA_TN = 1024
INPROJ_TM = 1024
INPROJ_TN = 512
ATT_TQ = 2048
WIN_GROUP = 8
NBR_TQ = 4096
NBR_GROUP = 64
OUT_TM = 1024
OUT_TC = 512
FFN_TM = 512
FFN_TF = 512
FFN_ACT_SPLIT = 2
HALO = 16
N_DR = 2 * NA_KH - 1
N_DC = 2 * NA_KW - 1


def _cparams(n_axes):
    return pltpu.CompilerParams(dimension_semantics=("arbitrary",) * n_axes,
                                vmem_limit_bytes=VMEM_LIMIT)


def _sigmoid(x):
    return 1.0 / (1.0 + jnp.exp(-x))


def _rms(x):
    return x * lax.rsqrt(jnp.mean(x * x, axis=-1, keepdims=True) + EPS)


def _ada_kernel(c_ref, w_ref, b_ref, o_ref):
    c = c_ref[...]
    c_act = (c * _sigmoid(c)).astype(BF16)
    o_ref[...] = jnp.dot(c_act, w_ref[...].astype(BF16), preferred_element_type=F32) + b_ref[...]


def _ada_call(c_pad, ada_w, ada_b):
    n_layers, d, six_d = ada_w.shape
    rows = c_pad.shape[0]
    return pl.pallas_call(
        _ada_kernel,
        grid=(n_layers, six_d // ADA_TN),
        in_specs=[
            pl.BlockSpec((rows, d), lambda l, n: (0, 0)),
            pl.BlockSpec((None, d, ADA_TN), lambda l, n: (l, 0, n)),
            pl.BlockSpec((None, 1, ADA_TN), lambda l, n: (l, 0, n)),
        ],
        out_specs=pl.BlockSpec((None, rows, ADA_TN), lambda l, n: (l, 0, n)),
        out_shape=jax.ShapeDtypeStruct((n_layers, rows, six_d), F32),
        compiler_params=_cparams(2),
        name="ada_mod",
    )(c_pad, ada_w, ada_b.reshape(n_layers, 1, six_d))


def _rope(a, cos, sin_up, sin_dn):
    half = ROT_DIM // 2
    return (a * cos + pltpu.roll(a, half, 1) * sin_up
            + pltpu.roll(a, HEAD_DIM - half, 1) * sin_dn)


def _inproj_kernel(x_ref, mod_ref, w_ref, rope_ref, gains_ref, o_ref, h_ref, acc0_ref, acc1_ref, *,
                   n_tiles):
    s = pl.program_id(0)
    n_steps = pl.num_programs(0) - 1
    heads_per_tile = INPROJ_TN // HEAD_DIM
    n_cur = jnp.minimum(s, n_steps - 1) % n_tiles
    n = jnp.maximum(s - 1, 0) % n_tiles
    slot = s % 2

    @pl.when((n_cur == 0) & (s < n_steps))
    def _():
        h_ref[...] = (_rms(x_ref[...]) * mod_ref[0:1, :] + mod_ref[1:2, :]).astype(BF16)

    @pl.when(s == 0)
    def _():
        acc1_ref[...] = jnp.zeros_like(acc1_ref)

    def put(k, val):
        o_ref[:, k * HEAD_DIM:(k + 1) * HEAD_DIM] = val.astype(BF16)

    def rope(a):
        return _rope(a, rope_ref[0], rope_ref[1], rope_ref[2])

    qna_ref, kna_ref, qnb_ref, knb_ref = [gains_ref.at[k:k + 1, :] for k in range(4)]

    n_qa = HA // heads_per_tile
    t_kva = n_qa
    t_qb = COL_QB // heads_per_tile
    t_kb = COL_KB // heads_per_tile
    t_vb = COL_VB // heads_per_tile
    t_gate = COL_GA // heads_per_tile

    def epi_qa(head):
        for k in range(heads_per_tile):
            put(k, rope(_rms(head(k)) * qna_ref[...]))

    def epi_kva(head):
        for k in range(GA):
            put(k, rope(_rms(head(k)) * kna_ref[...]))
        for k in range(GA, heads_per_tile):
            put(k, head(k))

    def epi_qb(head):
        for k in range(heads_per_tile):
            put(k, _rms(head(k)) * qnb_ref[...])

    def epi_kb(head):
        for k in range(heads_per_tile):
            put(k, _rms(head(k)) * knb_ref[...])

    def epi_vb(head):
        for k in range(heads_per_tile):
            put(k, head(k))

    def epi_gate(head):
        for k in range(heads_per_tile):
            put(k, _sigmoid(head(k)))

    kinds = [
        (n < n_qa, epi_qa),
        (n == t_kva, epi_kva),
        ((n >= t_qb) & (n < t_kb), epi_qb),
        ((n >= t_kb) & (n < t_vb), epi_kb),
        ((n >= t_vb) & (n < t_gate), epi_vb),
        (n >= t_gate, epi_gate),
    ]

    for parity, (wr_ref, rd_ref) in enumerate(((acc0_ref, acc1_ref), (acc1_ref, acc0_ref))):
        for cond, epilogue in kinds:
            @pl.when(cond & (slot == parity))
            def _(wr_ref=wr_ref, rd_ref=rd_ref, epilogue=epilogue):
                epilogue(lambda k: rd_ref[:, k * HEAD_DIM:(k + 1) * HEAD_DIM])
                wr_ref[...] = jnp.dot(h_ref[...], w_ref[...], preferred_element_type=F32)


def _inproj_call(x2, mod, w, layer, rope_tabs, gains, seq):
    t, d = x2.shape
    tm, tn = INPROJ_TM, INPROJ_TN
    assert 2 * GA * HEAD_DIM == tn and COL_QB * HEAD_DIM % tn == 0
    tiles_per_seq = seq // tm
    n_tiles = IN_COLS // tn
    n_steps = (t // tm) * n_tiles

    def cur(s):
        return jnp.minimum(s, n_steps - 1)

    def prev(s):
        return jnp.maximum(s - 1, 0)

    return pl.pallas_call(
        functools.partial(_inproj_kernel, n_tiles=n_tiles),
        grid=(n_steps + 1,),
        in_specs=[
            pl.BlockSpec((tm, d), lambda s: (cur(s) // n_tiles, 0)),
            pl.BlockSpec((None, mod.shape[1], d),
                         lambda s: (cur(s) // n_tiles // tiles_per_seq, 0, 0)),
            pl.BlockSpec((None, d, tn), lambda s: (layer, 0, cur(s) % n_tiles)),
            pl.BlockSpec((rope_tabs.shape[0], tm, HEAD_DIM),
                         lambda s: (0, (prev(s) // n_tiles) % tiles_per_seq, 0)),
            pl.BlockSpec(gains.shape, lambda s: (0, 0)),
        ],
        out_specs=pl.BlockSpec((tm, tn), lambda s: (prev(s) // n_tiles, prev(s) % n_tiles)),
        out_shape=jax.ShapeDtypeStruct((t, IN_COLS), BF16),
        scratch_shapes=[pltpu.VMEM((tm, d), BF16), pltpu.VMEM((tm, tn), F32),
                        pltpu.VMEM((tm, tn), F32)],
        compiler_params=_cparams(1),
        name="in_proj",
    )(x2, mod, w, rope_tabs, gains)


def _win_kernel(sink_ref, q_ref, k_ref, v_ref, o_ref, mask_ref, *, seq):
    b = pl.program_id(0)
    g = pl.program_id(1)
    i = pl.program_id(2)
    band = 3 * WIN
    rows = RA * WIN
    row = lax.broadcasted_iota(jnp.int32, (rows, 1), 0)

    @pl.when((b == 0) & (g == 0) & (i == 0))
    def _():
        rel0 = lax.broadcasted_iota(jnp.int32, (1, band), 1) - (row & (WIN - 1))
        for kind in range(3):
            mask_ref[kind] = jnp.where(jnp.abs(rel0 - kind * WIN) <= WIN, 0.0, NEG_INF)

    snk = jnp.full((rows, 1), sink_ref[g * RA + RA - 1], F32)
    for r in range(RA - 2, -1, -1):
        snk = jnp.where(row < (r + 1) * WIN, sink_ref[g * RA + r], snk)
    n_blocks = ATT_TQ // WIN
    for j0 in range(0, n_blocks, WIN_GROUP):
        group = range(j0, min(j0 + WIN_GROUP, n_blocks))
        starts, scores, probs, denoms = [], [], [], []
        for j in group:
            blk = i * n_blocks + j
            start = pl.multiple_of(jnp.clip((blk - 1) * WIN, 0, seq - band), WIN)
            starts.append(start)
            qs = jnp.concatenate(
                [q_ref[j * WIN:(j + 1) * WIN, r * HEAD_DIM:(r + 1) * HEAD_DIM] for r in range(RA)],
                axis=0)
            s = lax.dot_general(qs, k_ref[pl.ds(start, band), :], (((1,), (1,)), ((), ())),
                                preferred_element_type=F32)
            scores.append(s + mask_ref[(blk * WIN - start) // WIN])
        for s in scores:
            m = jnp.maximum(jnp.max(s, axis=-1, keepdims=True), snk)
            p = jnp.exp(s - m)
            denoms.append(jnp.sum(p, axis=-1, keepdims=True) + jnp.exp(snk - m))
            probs.append(p.astype(BF16))
        for idx, j in enumerate(group):
            o = jnp.dot(probs[idx], v_ref[pl.ds(starts[idx], band), :],
                        preferred_element_type=F32) / denoms[idx]
            for r in range(RA):
                o_ref[j * WIN:(j + 1) * WIN, r * HEAD_DIM:(r + 1) * HEAD_DIM] = (
                    o[r * WIN:(r + 1) * WIN].astype(BF16))


def _win_call(qkvg, sink, batch, seq):
    t = qkvg.shape[0]
    tq = ATT_TQ
    per_seq = seq // tq
    gw = RA * HEAD_DIM
    return pl.pallas_call(
        functools.partial(_win_kernel, seq=seq),
        grid=(batch, GA, per_seq),
        in_specs=[
            pl.BlockSpec(memory_space=pltpu.SMEM),
            pl.BlockSpec((tq, gw), lambda b, g, i: (b * per_seq + i, g)),
            pl.BlockSpec((seq, HEAD_DIM), lambda b, g, i: (b, COL_KA + g)),
            pl.BlockSpec((seq, HEAD_DIM), lambda b, g, i: (b, COL_VA + g)),
        ],
        out_specs=pl.BlockSpec((tq, gw), lambda b, g, i: (b * per_seq + i, g)),
        out_shape=jax.ShapeDtypeStruct((t, HA * HEAD_DIM), BF16),
        scratch_shapes=[pltpu.VMEM((3, RA * WIN, 3 * WIN), F32)],
        compiler_params=_cparams(3),
        name="win_attn",
    )(sink, qkvg, qkvg, qkvg)


def _nbr_kernel(tab_ref, q_ref, k_ref, v_ref, o_ref, bias_ref, tz_ref, *, grid_rows):
    h = pl.program_id(0)
    b = pl.program_id(1)
    i = pl.program_id(2)
    w = GRID_W
    keys = NA_KH * w

    @pl.when((b == 0) & (i == 0))
    def _():
        qi = lax.broadcasted_iota(jnp.int32, (w, 2 * w), 0)
        lane = lax.broadcasted_iota(jnp.int32, (w, 2 * w), 1)
        kc = lane & (w - 1)
        diff = kc - qi
        c0 = jnp.clip(qi - NA_KW // 2, 0, w - NA_KW)
        valid = (kc >= c0) & (kc < c0 + NA_KW)
        base = h * (N_DR * N_DC)
        for dr in range(N_DR):
            tz = jnp.zeros((w, 2 * w), F32)
            for dc in range(N_DC):
                tz = jnp.where(diff == dc - (NA_KW - 1), tab_ref[base + dr * N_DC + dc], tz)
            tz_ref[dr] = jnp.where(valid, tz, NEG_INF)
        low = lane < w
        for delta in range(NA_KH):
            for c in range(NA_KH // 2):
                bias_ref[delta, :, c * 2 * w:(c + 1) * 2 * w] = jnp.where(
                    low, tz_ref[2 * c - delta + NA_KH - 1], tz_ref[2 * c + 1 - delta + NA_KH - 1])

    n_rows = NBR_TQ // w
    for g0 in range(0, n_rows, NBR_GROUP):
        group = range(g0, min(g0 + NBR_GROUP, n_rows))
        offs, scores, probs, denoms = [], [], [], []
        for rr in group:
            r = i * n_rows + rr
            r0 = jnp.clip(r - NA_KH // 2, 0, grid_rows - NA_KH)
            off = pl.multiple_of(r0 * w, w)
            offs.append(off)
            s = lax.dot_general(q_ref[rr * w:(rr + 1) * w, :], k_ref[pl.ds(off, keys), :],
                                (((1,), (1,)), ((), ())), preferred_element_type=F32)
            scores.append(s + bias_ref[r - r0])
        for s in scores:
            p = jnp.exp(s - jnp.max(s, axis=-1, keepdims=True))
            denoms.append(jnp.sum(p, axis=-1, keepdims=True))
            probs.append(p.astype(BF16))
        for idx, rr in enumerate(group):
            o = jnp.dot(probs[idx], v_ref[pl.ds(offs[idx], keys), :], preferred_element_type=F32)
            o_ref[rr * w:(rr + 1) * w, :] = (o / denoms[idx]).astype(BF16)


def _nbr_call(qkvg, tab_flat, batch, seq):
    t = qkvg.shape[0]
    tq = NBR_TQ
    per_seq = seq // tq
    grid_rows = seq // GRID_W
    assert grid_rows >= NA_KH
    return pl.pallas_call(
        functools.partial(_nbr_kernel, grid_rows=grid_rows),
        grid=(HB, batch, per_seq),
        in_specs=[
            pl.BlockSpec(memory_space=pltpu.SMEM),
            pl.BlockSpec((tq, HEAD_DIM), lambda h, b, i: (b * per_seq + i, COL_QB + h)),
            pl.BlockSpec((seq, HEAD_DIM), lambda h, b, i: (b, COL_KB + h)),
            pl.BlockSpec((seq, HEAD_DIM), lambda h, b, i: (b, COL_VB + h)),
        ],
        out_specs=pl.BlockSpec((tq, HEAD_DIM), lambda h, b, i: (b * per_seq + i, h)),
        out_shape=jax.ShapeDtypeStruct((t, HB * HEAD_DIM), BF16),
        scratch_shapes=[pltpu.VMEM((NA_KH, GRID_W, NA_KH * GRID_W), F32),
                        pltpu.VMEM((N_DR, GRID_W, 2 * GRID_W), F32)],
        compiler_params=_cparams(3),
        name="nbr_attn",
    )(tab_flat, qkvg, qkvg, qkvg)


def _outproj_kernel(oa_ref, ob_ref, ga_ref, gb_ref, wpa_ref, wpb_ref, wo_ref, x_ref, gt_ref,
                    o_ref):
    c = pl.program_id(1)

    @pl.when(c == 0)
    def _():
        o_ref[...] = jnp.zeros_like(o_ref)

    ya = jnp.dot(oa_ref[...], wpa_ref[...], preferred_element_type=F32)
    yb = jnp.dot(ob_ref[...], wpb_ref[...], preferred_element_type=F32)
    merged = (ga_ref[...].astype(F32) * ya + gb_ref[...].astype(F32) * yb).astype(BF16)
    o_ref[...] += jnp.dot(merged, wo_ref[...], preferred_element_type=F32)

    @pl.when(c == pl.num_programs(1) - 1)
    def _():
        o_ref[...] = x_ref[...] + gt_ref[...] * o_ref[...]


def _outproj_call(oa, ob, qkvg, wpa, wpb, wo, layer, x2, gt, seq):
    t, d = x2.shape
    tm, tc = OUT_TM, OUT_TC
    ga0 = COL_GA * HEAD_DIM // tc
    gb0 = COL_GB * HEAD_DIM // tc
    assert ga0 * tc == COL_GA * HEAD_DIM and gb0 * tc == COL_GB * HEAD_DIM
    tiles_per_seq = seq // tm
    return pl.pallas_call(
        _outproj_kernel,
        grid=(t // tm, d // tc),
        in_specs=[
            pl.BlockSpec((tm, oa.shape[1]), lambda i, c: (i, 0)),
            pl.BlockSpec((tm, ob.shape[1]), lambda i, c: (i, 0)),
            pl.BlockSpec((tm, tc), lambda i, c: (i, ga0 + c)),
            pl.BlockSpec((tm, tc), lambda i, c: (i, gb0 + c)),
            pl.BlockSpec((None, wpa.shape[1], tc), lambda i, c: (layer, 0, c)),
            pl.BlockSpec((None, wpb.shape[1], tc), lambda i, c: (layer, 0, c)),
            pl.BlockSpec((None, tc, d), lambda i, c: (layer, c, 0)),
            pl.BlockSpec((tm, d), lambda i, c: (i, 0)),
            pl.BlockSpec((None, 1, d), lambda i, c: (i // tiles_per_seq, 0, 0)),
        ],
        out_specs=pl.BlockSpec((tm, d), lambda i, c: (i, 0)),
        out_shape=jax.ShapeDtypeStruct((t, d), F32),
        compiler_params=_cparams(2),
        name="out_proj",
    )(oa, ob, qkvg, qkvg, wpa, wpb, wo, x2, gt)


def _ffn_kernel(x_ref, xp_ref, xn_ref, mod_ref, wg_ref, wv_ref, cg_ref, cv_ref, wd_ref,
                o_ref, h_ref, acc_ref, u_ref, *, tiles_per_seq):
    i = pl.program_id(0)
    j = pl.program_id(1)
    tm = FFN_TM
    ext = tm + 2 * HALO
    n_taps = cg_ref.shape[0] - 1

    def normalise():
        def modulated(x):
            return _rms(x) * mod_ref[0:1, :] + mod_ref[1:2, :]

        pos = i % tiles_per_seq
        keep_prev = jnp.where(pos == 0, 0.0, 1.0)
        keep_next = jnp.where(pos == tiles_per_seq - 1, 0.0, 1.0)
        h_ref[0:HALO, :] = (modulated(xp_ref[...]) * keep_prev).astype(BF16)
        h_ref[HALO:HALO + tm, :] = modulated(x_ref[...]).astype(BF16)
        h_ref[HALO + tm:ext, :] = (modulated(xn_ref[...]) * keep_next).astype(BF16)

    rows = tm // FFN_ACT_SPLIT

    def conv(slot, c_ref, r0):
        base = HALO + r0
        return (u_ref[slot, base - 1:base - 1 + rows, :] * c_ref[0:1, :]
                + u_ref[slot, base:base + rows, :] * c_ref[1:2, :]
                + u_ref[slot, base + 1:base + 1 + rows, :] * c_ref[2:3, :]
                + c_ref[n_taps:n_taps + 1, :])

    def part():
        h = h_ref[...]
        u_ref[0] = jnp.dot(h, wg_ref[...], preferred_element_type=F32)
        u_ref[1] = jnp.dot(h, wv_ref[...], preferred_element_type=F32)
        acts = []
        for r0 in range(0, tm, rows):
            gate = conv(0, cg_ref, r0)
            val = conv(1, cv_ref, r0)
            acts.append((gate * _sigmoid(gate) * val).astype(BF16))
        act = jnp.concatenate(acts, axis=0)
        return jnp.dot(act, wd_ref[...], preferred_element_type=F32)

    last = pl.num_programs(1) - 1

    @pl.when(j == 0)
    def _():
        normalise()
        acc_ref[...] = part()

    @pl.when((j > 0) & (j < last))
    def _():
        acc_ref[...] += part()

    @pl.when(j == last)
    def _():
        o_ref[...] = x_ref[...] + mod_ref[2:3, :] * (acc_ref[...] + part())


def _ffn_call(x2, mod, w_up, conv_p, w_down, layer, seq):
    t, d = x2.shape
    tm, tf = FFN_TM, FFN_TF
    d_ff = w_down.shape[1]
    n_rows = conv_p.shape[1]
    assert n_rows == 4
    nf = d_ff // tf
    assert nf >= 2
    tiles_per_seq = seq // tm
    halo_per_tile = tm // HALO
    n_halo = t // HALO
    return pl.pallas_call(
        functools.partial(_ffn_kernel, tiles_per_seq=tiles_per_seq),
        grid=(t // tm, nf),
        in_specs=[
            pl.BlockSpec((tm, d), lambda i, j: (i, 0)),
            pl.BlockSpec((HALO, d), lambda i, j: (jnp.maximum(i * halo_per_tile - 1, 0), 0)),
            pl.BlockSpec((HALO, d),
                         lambda i, j: (jnp.minimum((i + 1) * halo_per_tile, n_halo - 1), 0)),
            pl.BlockSpec((None, mod.shape[1], d), lambda i, j: (i // tiles_per_seq, 0, 0)),
            pl.BlockSpec((None, d, tf), lambda i, j: (layer, 0, j)),
            pl.BlockSpec((None, d, tf), lambda i, j: (layer, 0, j + nf)),
            pl.BlockSpec((None, n_rows, tf), lambda i, j: (layer, 0, j)),
            pl.BlockSpec((None, n_rows, tf), lambda i, j: (layer, 0, j + nf)),
            pl.BlockSpec((None, tf, d), lambda i, j: (layer, j, 0)),
        ],
        out_specs=pl.BlockSpec((tm, d), lambda i, j: (i, 0)),
        out_shape=jax.ShapeDtypeStruct((t, d), F32),
        scratch_shapes=[pltpu.VMEM((tm + 2 * HALO, d), BF16), pltpu.VMEM((tm, d), F32),
                        pltpu.VMEM((2, tm + 2 * HALO, tf), F32)],
        compiler_params=_cparams(2),
        name="conv_ffn",
    )(x2, x2, x2, mod, w_up, w_up, conv_p, conv_p, w_down)


def _rope_tables(seq):
    half = ROT_DIM // 2
    pos = jnp.arange(seq, dtype=F32)
    inv = jnp.float32(ROPE_THETA) ** (-jnp.arange(0, ROT_DIM, 2, dtype=F32) / ROT_DIM)
    ang = pos[:, None] * inv[None, :]
    cos, sin = jnp.cos(ang), jnp.sin(ang)
    rest = HEAD_DIM - ROT_DIM
    cos_t = jnp.concatenate([cos, cos, jnp.ones((seq, rest), F32)], axis=1)
    sin_up = jnp.concatenate([jnp.zeros((seq, half), F32), sin, jnp.zeros((seq, rest), F32)], axis=1)
    sin_dn = jnp.concatenate([-sin, jnp.zeros((seq, HEAD_DIM - half), F32)], axis=1)
    return jnp.stack([cos_t, sin_up, sin_dn])


def kernel(x, c, ada_w, ada_b, norm_mix, norm_ffn, w_in, qn_a, kn_a, qn_b, kn_b, sink_a,
           rel_bias_b, w_proj_a, w_proj_b, w_out, w_up, conv_w, conv_b, w_down):
    batch, seq, d = x.shape
    n_layers = ada_w.shape[0]
    t = batch * seq
    x2 = x.reshape(t, d)

    c_pad = jnp.pad(c, ((0, 8 - batch % 8 if batch % 8 else 0), (0, 0)))
    mod = _ada_call(c_pad, ada_w, ada_b)[:, :batch]
    mod = mod.reshape(n_layers, batch, 6, d)
    rope_tabs = _rope_tables(seq)

    w_in_b, w_pa_b, w_pb_b, w_out_b, w_up_b, w_down_b = [
        w.astype(BF16) for w in (w_in, w_proj_a, w_proj_b, w_out, w_up, w_down)]
    conv_p = jnp.concatenate([conv_w, conv_b[:, None, :]], axis=1)

    for l in range(n_layers):
        sh_a, sc_a, gt_a, sh_m, sc_m, gt_m = [mod[l, :, k] for k in range(6)]
        mod_mix = jnp.stack([norm_mix[l] * (1.0 + sc_a), sh_a], axis=1)
        mod_ffn = jnp.stack([norm_ffn[l] * (1.0 + sc_m), sh_m, gt_m], axis=1)
        gains = jnp.stack([qn_a[l] * QK_SCALE, kn_a[l], qn_b[l] * QK_SCALE, kn_b[l]])
        qkvg = _inproj_call(x2, mod_mix, w_in_b, l, rope_tabs, gains, seq)
        oa = _win_call(qkvg, sink_a[l], batch, seq)
        ob = _nbr_call(qkvg, rel_bias_b[l].reshape(-1), batch, seq)
        x2 = _outproj_call(oa, ob, qkvg, w_pa_b, w_pb_b, w_out_b, l, x2, gt_a[:, None, :], seq)
        x2 = _ffn_call(x2, mod_ffn, w_up_b, conv_p, w_down_b, l, seq)
    return x2.reshape(batch, seq, d)
```
